```python
import math
import jax, jax.numpy as jnp
from jax import lax
import numpy as np

D_MODEL = 2048
BATCH = 2
SEQ = 4096
DEPTH = 1

HEAD_DIM = 64
N_Q_HEADS = 16
N_KV_HEADS = 2
Q_PER_KV = N_Q_HEADS // N_KV_HEADS
ATTN_WIDTH = N_Q_HEADS * HEAD_DIM
KV_WIDTH = N_KV_HEADS * HEAD_DIM
WINDOW = 128
BLOCK = 128
NEG_INF = -1e30
SSM_WIDTH = D_MODEL // 2
SSM_GROUP = 16
N_SSM_GROUPS = SSM_WIDTH // SSM_GROUP
SSM_STATE = 64
DT_MIN = 1e-3
DT_MAX = 1e-1
D_FF = 5632
CONV_WIDTH = 3
N_BRANCHES = 2
IN_WIDTH = ATTN_WIDTH + 2 * KV_WIDTH + SSM_WIDTH + N_BRANCHES * D_MODEL
RMS_EPS = 1e-6

kernel_name = 'hybrid_swa_s5_convffn_adaln'


def rmsnorm(x, g):
    xf = x.astype(jnp.float32)
    xf = xf * lax.rsqrt(jnp.mean(xf * xf, axis=-1, keepdims=True) + RMS_EPS)
    return xf.astype(x.dtype) * g


def sliding_window_attention(q, k, v, sinks):
    b, l = q.shape[0], q.shape[1]
    nb = l // BLOCK
    qb = q.reshape(b, nb, BLOCK, N_KV_HEADS, Q_PER_KV, HEAD_DIM)
    pad = ((0, 0), (BLOCK, 0), (0, 0), (0, 0))
    kp = jnp.pad(k, pad).reshape(b, nb + 1, BLOCK, N_KV_HEADS, HEAD_DIM)
    vp = jnp.pad(v, pad).reshape(b, nb + 1, BLOCK, N_KV_HEADS, HEAD_DIM)
    kw = jnp.concatenate([kp[:, :-1], kp[:, 1:]], axis=2)
    vw = jnp.concatenate([vp[:, :-1], vp[:, 1:]], axis=2)
    s = jnp.einsum('bnqhgd,bnkhd->bnhgqk', qb, kw).astype(jnp.float32) * (HEAD_DIM ** -0.5)
    qi = jnp.arange(BLOCK)[:, None]
    kj = jnp.arange(2 * BLOCK)[None, :]
    rel = qi + BLOCK - kj
    band = (rel >= 0) & (rel < WINDOW)
    key_pos = jnp.arange(nb)[:, None, None] * BLOCK - BLOCK + kj[None]
    mask = band[None] & (key_pos >= 0)
    s = jnp.where(mask[None, :, None, None], s, NEG_INF)
    sink = sinks.astype(jnp.float32).reshape(N_KV_HEADS, Q_PER_KV)[None, None, :, :, None, None]
    m = jnp.maximum(jnp.max(s, axis=-1, keepdims=True), sink)
    p = jnp.exp(s - m)
    p = p / (jnp.sum(p, axis=-1, keepdims=True) + jnp.exp(sink - m))
    o = jnp.einsum('bnhgqk,bnkhd->bnqhgd', p.astype(v.dtype), vw)
    return o.reshape(b, l, ATTN_WIDTH)


def s5_layer(u, a_re, a_im, log_dt, b_re, b_im, c_re, c_im, d_skip):
    bsz, l = u.shape[0], u.shape[1]
    f32 = jnp.float32
    ug = u.reshape(bsz, l, N_SSM_GROUPS, SSM_GROUP).astype(f32)
    dt = jnp.exp(log_dt.astype(f32))[:, None]
    ar, ai = a_re.astype(f32), a_im.astype(f32)
    mag = jnp.exp(ar * dt)
    lr, li = mag * jnp.cos(ai * dt), mag * jnp.sin(ai * dt)
    den = ar * ar + ai * ai
    zr = ((lr - 1.0) * ar + li * ai) / den
    zi = (li * ar - (lr - 1.0) * ai) / den
    br, bi = b_re.astype(f32), b_im.astype(f32)
    bbar_r = zr[:, :, None] * br - zi[:, :, None] * bi
    bbar_i = zr[:, :, None] * bi + zi[:, :, None] * br
    xr = jnp.einsum('blgp,gnp->blgn', ug, bbar_r)
    xi = jnp.einsum('blgp,gnp->blgn', ug, bbar_i)
    seq_r = jnp.broadcast_to(lr[None, None], (1, l, N_SSM_GROUPS, SSM_STATE))
    seq_i = jnp.broadcast_to(li[None, None], (1, l, N_SSM_GROUPS, SSM_STATE))

    def combine(e1, e2):
        a1r, a1i, b1r, b1i = e1
        a2r, a2i, b2r, b2i = e2
        return (a1r * a2r - a1i * a2i,
                a1r * a2i + a1i * a2r,
                a2r * b1r - a2i * b1i + b2r,
                a2r * b1i + a2i * b1r + b2i)

    _, _, hr, hi = lax.associative_scan(combine, (seq_r, seq_i, xr, xi), axis=1)
    y = (jnp.einsum('blgn,gpn->blgp', hr, c_re.astype(f32))
         - jnp.einsum('blgn,gpn->blgp', hi, c_im.astype(f32))
         + d_skip.astype(f32) * ug)
    return y.reshape(bsz, l, SSM_WIDTH).astype(u.dtype)


def conv_ffn(h, w_up, conv_w, conv_b, w_down):
    up = h @ w_up
    gate, val = jnp.split(up, 2, axis=-1)
    gate = lax.conv_general_dilated(
        gate, conv_w[:, None, :], window_strides=(1,), padding=((CONV_WIDTH - 1, 0),),
        dimension_numbers=('NWC', 'WIO', 'NWC'), feature_group_count=D_FF) + conv_b
    return (jax.nn.silu(gate) * val) @ w_down


def setup_inputs(seed: int = 0) -> dict:
    key = jax.random.key(seed)
    ks = jax.random.split(key, 24)
    f32 = jnp.float32

    def nrm(k, shape, scale):
        return jax.random.normal(k, shape, f32) * scale

    G, N, P = N_SSM_GROUPS, SSM_STATE, SSM_GROUP
    a_im0 = jnp.pi * jnp.arange(N, dtype=f32)
    return {
        'x': nrm(ks[0], (BATCH, SEQ, D_MODEL), 1.0),
        'c': nrm(ks[1], (BATCH, D_MODEL), 1.0),
        'ada_w': nrm(ks[2], (DEPTH, D_MODEL, 6 * D_MODEL), D_MODEL ** -0.5),
        'ada_b': nrm(ks[3], (DEPTH, 6 * D_MODEL), 0.01),
        'norm_mix_g': 1.0 + nrm(ks[4], (DEPTH, D_MODEL), 0.01),
        'w_in': nrm(ks[5], (DEPTH, D_MODEL, IN_WIDTH), D_MODEL ** -0.5),
        'attn_sinks': nrm(ks[6], (DEPTH, N_Q_HEADS), 0.5),
        'w_attn_proj': nrm(ks[7], (DEPTH, ATTN_WIDTH, D_MODEL), ATTN_WIDTH ** -0.5),
        'ssm_a_re': -0.5 + nrm(ks[8], (DEPTH, G, N), 0.01),
        'ssm_a_im': a_im0 + nrm(ks[9], (DEPTH, G, N), 0.01),
        'ssm_log_dt': jax.random.uniform(ks[10], (DEPTH, G), f32, math.log(DT_MIN), math.log(DT_MAX)),
        'ssm_b_re': nrm(ks[11], (DEPTH, G, N, P), (2 * P) ** -0.5),
        'ssm_b_im': nrm(ks[12], (DEPTH, G, N, P), (2 * P) ** -0.5),
        'ssm_c_re': nrm(ks[13], (DEPTH, G, P, N), N ** -0.5),
        'ssm_c_im': nrm(ks[14], (DEPTH, G, P, N), N ** -0.5),
        'ssm_d': nrm(ks[15], (DEPTH, G, P), 1.0),
        'w_ssm_glu': nrm(ks[16], (DEPTH, SSM_WIDTH, 2 * D_MODEL), SSM_WIDTH ** -0.5),
        'w_out': nrm(ks[17], (DEPTH, D_MODEL, D_MODEL), D_MODEL ** -0.5),
        'norm_ffn_g': 1.0 + nrm(ks[18], (DEPTH, D_MODEL), 0.01),
        'w_ffn_up': nrm(ks[19], (DEPTH, D_MODEL, 2 * D_FF), D_MODEL ** -0.5),
        'ffn_conv_w': nrm(ks[20], (DEPTH, CONV_WIDTH, D_FF), CONV_WIDTH ** -0.5),
        'ffn_conv_b': nrm(ks[21], (DEPTH, D_FF), 0.01),
        'w_ffn_down': nrm(ks[22], (DEPTH, D_FF, D_MODEL), D_FF ** -0.5),
        'final_g': 1.0 + nrm(ks[23], (D_MODEL,), 0.01),
    }


def reference(x, c, ada_w, ada_b, norm_mix_g, w_in, attn_sinks, w_attn_proj,
              ssm_a_re, ssm_a_im, ssm_log_dt, ssm_b_re, ssm_b_im, ssm_c_re, ssm_c_im, ssm_d,
              w_ssm_glu, w_out, norm_ffn_g, w_ffn_up, ffn_conv_w, ffn_conv_b, w_ffn_down, final_g):
    bsz, l = x.shape[0], x.shape[1]
    cond = jax.nn.silu(c)
    splits = [ATTN_WIDTH, ATTN_WIDTH + KV_WIDTH, ATTN_WIDTH + 2 * KV_WIDTH,
              ATTN_WIDTH + 2 * KV_WIDTH + SSM_WIDTH, ATTN_WIDTH + 2 * KV_WIDTH + SSM_WIDTH + D_MODEL]
    for i in range(DEPTH):
        mod = (cond @ ada_w[i] + ada_b[i])[:, None, :]
        sh1, sc1, g1, sh2, sc2, g2 = jnp.split(mod, 6, axis=-1)
        h = rmsnorm(x, norm_mix_g[i]) * (1.0 + sc1) + sh1
        proj = h @ w_in[i]
        q, k, v, u, g_attn, g_ssm = jnp.split(proj, splits, axis=-1)
        q = q.reshape(bsz, l, N_Q_HEADS, HEAD_DIM)
        k = k.reshape(bsz, l, N_KV_HEADS, HEAD_DIM)
        v = v.reshape(bsz, l, N_KV_HEADS, HEAD_DIM)
        attn = sliding_window_attention(q, k, v, attn_sinks[i]) @ w_attn_proj[i]
        y = s5_layer(u, ssm_a_re[i], ssm_a_im[i], ssm_log_dt[i], ssm_b_re[i], ssm_b_im[i],
                     ssm_c_re[i], ssm_c_im[i], ssm_d[i])
        glu_a, glu_b = jnp.split(jax.nn.gelu(y) @ w_ssm_glu[i], 2, axis=-1)
        ssm = glu_a * jax.nn.sigmoid(glu_b)
        mixed = jax.nn.sigmoid(g_attn) * attn + jax.nn.sigmoid(g_ssm) * ssm
        x = x + g1 * (mixed @ w_out[i])
        h = rmsnorm(x, norm_ffn_g[i]) * (1.0 + sc2) + sh2
        x = x + g2 * conv_ffn(h, w_ffn_up[i], ffn_conv_w[i], ffn_conv_b[i], w_ffn_down[i])
    return rmsnorm(x, final_g)
```

```python
import functools

import jax
import jax.numpy as jnp
import numpy as np
from jax import lax
from jax.experimental import pallas as pl
from jax.experimental.pallas import tpu as pltpu

F32 = jnp.float32
BF16 = jnp.bfloat16

D_MODEL = 2048
HEAD_DIM = 64
N_Q_HEADS = 16
N_KV_HEADS = 2
Q_PER_KV = N_Q_HEADS // N_KV_HEADS
ATTN_WIDTH = N_Q_HEADS * HEAD_DIM
KV_WIDTH = N_KV_HEADS * HEAD_DIM
QKV_WIDTH = ATTN_WIDTH + 2 * KV_WIDTH
WINDOW = 128
BLOCK = 128
NEG_INF = -1e30
SSM_WIDTH = D_MODEL // 2
SSM_GROUP = 16
N_SSM_GROUPS = SSM_WIDTH // SSM_GROUP
SSM_STATE = 64
N_STATES = N_SSM_GROUPS * SSM_STATE
D_FF = 5632
RMS_EPS = 1e-6

V7X_SUBLANES = 8
V7X_LANES = 128
STATE_ROWS = N_STATES // V7X_LANES
CH_BLOCKS = SSM_WIDTH // V7X_LANES
GROUPS_PER_BLOCK = V7X_LANES // SSM_GROUP
STATES_PER_BLOCK = GROUPS_PER_BLOCK * SSM_STATE
TILES_PER_BLOCK = STATES_PER_BLOCK // V7X_LANES


def _cparams(sem, vmem_mb):
    return pltpu.CompilerParams(dimension_semantics=sem, vmem_limit_bytes=vmem_mb * 1024 * 1024)


def _sigmoid(x):
    return 1.0 / (1.0 + jnp.exp(-x))


def _silu(x):
    return x * _sigmoid(x)


def _gelu_tanh(x):
    c = np.float32(np.sqrt(2.0 / np.pi))
    return x * (0.5 * (1.0 + jnp.tanh(c * (x + np.float32(0.044715) * (x * x * x)))))


def _rms(x):
    return x * lax.rsqrt(jnp.mean(x * x, axis=-1, keepdims=True) + RMS_EPS)


def _modulated_norm(x, g, scale, shift):
    return _rms(x) * g * (1.0 + scale) + shift


def _ada_kernel(c_ref, w_ref, b_ref, o_ref):
    cond = _silu(c_ref[...]).astype(BF16)
    o_ref[...] = jnp.dot(cond, w_ref[...].astype(BF16), preferred_element_type=F32) + b_ref[...]


def _ada(c_pad, w, b):
    n = w.shape[1]
    tn = 1024
    return pl.pallas_call(
        _ada_kernel,
        grid=(n // tn,),
        in_specs=[pl.BlockSpec((c_pad.shape[0], D_MODEL), lambda j: (0, 0)),
                  pl.BlockSpec((D_MODEL, tn), lambda j: (0, j)),
                  pl.BlockSpec((1, tn), lambda j: (0, j))],
        out_specs=pl.BlockSpec((c_pad.shape[0], tn), lambda j: (0, j)),
        out_shape=jax.ShapeDtypeStruct((c_pad.shape[0], n), F32),
        compiler_params=_cparams(("arbitrary",), 40),
        name="ada",
    )(c_pad, w, b)


def _inproj_kernel(x_ref, mod_ref, g_ref, w_ref, qkv_ref, u_ref):
    h = _modulated_norm(x_ref[...], g_ref[...], mod_ref[1:2, :], mod_ref[0:1, :]).astype(BF16)
    p = jnp.dot(h, w_ref[...], preferred_element_type=F32)
    qkv_ref[...] = p[:, :QKV_WIDTH].astype(BF16)
    u_ref[...] = p[:, QKV_WIDTH:]


def _inproj(x2, mod, g, w, seq):
    t = x2.shape[0]
    tm = 512
    tpb = seq // tm
    n = w.shape[1]
    return pl.pallas_call(
        _inproj_kernel,
        grid=(t // tm,),
        in_specs=[pl.BlockSpec((tm, D_MODEL), lambda i: (i, 0)),
                  pl.BlockSpec((None, 6, D_MODEL), lambda i: (i // tpb, 0, 0)),
                  pl.BlockSpec((1, D_MODEL), lambda i: (0, 0)),
                  pl.BlockSpec((D_MODEL, n), lambda i: (0, 0))],
        out_specs=[pl.BlockSpec((tm, QKV_WIDTH), lambda i: (i, 0)),
                   pl.BlockSpec((tm, SSM_WIDTH), lambda i: (i, 0))],
        out_shape=[jax.ShapeDtypeStruct((t, QKV_WIDTH), BF16),
                   jax.ShapeDtypeStruct((t, SSM_WIDTH), F32)],
        compiler_params=_cparams(("arbitrary",), 48),
        name="inproj",
    )(x2, mod, g, w)


def _attn_kernel(sink_ref, q_ref, kvc_ref, kvp_ref, o_ref, *, tq, tiles_per_seq):
    first = (pl.program_id(0) % tiles_per_seq) == 0
    qi = lax.broadcasted_iota(jnp.int32, (BLOCK, 2 * BLOCK), 0)
    kj = lax.broadcasted_iota(jnp.int32, (BLOCK, 2 * BLOCK), 1)
    rel = qi + BLOCK - kj
    band = (rel >= 0) & (rel < WINDOW)
    band_first = band & (kj >= jnp.where(first, BLOCK, 0))
    scale = np.float32(HEAD_DIM ** -0.5)
    for qb in range(tq // BLOCK):
        r0 = qb * BLOCK
        if qb == 0:
            prev = kvp_ref[...]
            mask = band_first
        else:
            prev = kvc_ref[r0 - BLOCK:r0, :]
            mask = band
        win = jnp.concatenate([prev, kvc_ref[r0:r0 + BLOCK, :]], axis=0)
        outs = []
        for g in range(N_KV_HEADS):
            kg = win[:, g * HEAD_DIM:(g + 1) * HEAD_DIM]
            vg = win[:, KV_WIDTH + g * HEAD_DIM:KV_WIDTH + (g + 1) * HEAD_DIM]
            for hh in range(Q_PER_KV):
                head = g * Q_PER_KV + hh
                qh = q_ref[r0:r0 + BLOCK, head * HEAD_DIM:(head + 1) * HEAD_DIM]
                s = lax.dot_general(qh, kg, (((1,), (1,)), ((), ())), preferred_element_type=F32) * scale
                s = jnp.where(mask, s, NEG_INF)
                sink = sink_ref[head]
                m = jnp.maximum(jnp.max(s, axis=-1, keepdims=True), sink)
                p = jnp.exp(s - m)
                denom = jnp.sum(p, axis=-1, keepdims=True) + jnp.exp(sink - m)
                pv = jnp.dot(p.astype(BF16), vg, preferred_element_type=F32)
                outs.append(pv * (1.0 / denom))
        o_ref[r0:r0 + BLOCK, :] = jnp.concatenate(outs, axis=-1).astype(BF16)


def _attention(sinks, qkv, seq):
    t = qkv.shape[0]
    tq = 512
    tiles_per_seq = seq // tq
    kv_col = ATTN_WIDTH // (2 * KV_WIDTH)
    rb = tq // BLOCK
    return pl.pallas_call(
        functools.partial(_attn_kernel, tq=tq, tiles_per_seq=tiles_per_seq),
        grid=(t // tq,),
        in_specs=[pl.BlockSpec(memory_space=pltpu.SMEM),
                  pl.BlockSpec((tq, ATTN_WIDTH), lambda i: (i, 0)),
                  pl.BlockSpec((tq, 2 * KV_WIDTH), lambda i: (i, kv_col)),
                  pl.BlockSpec((BLOCK, 2 * KV_WIDTH), lambda i: (jnp.maximum(i * rb - 1, 0), kv_col))],
        out_specs=pl.BlockSpec((tq, ATTN_WIDTH), lambda i: (i, 0)),
        out_shape=jax.ShapeDtypeStruct((t, ATTN_WIDTH), BF16),
        compiler_params=_cparams(("arbitrary",), 32),
        name="attn",
    )(sinks, qkv, qkv, qkv)


def _ssmprep_kernel(ar_ref, ai_ref, ldt_ref, br_ref, bi_ref, lr_ref, li_ref, bbr_ref, bbi_ref):
    ar = ar_ref[...]
    ai = ai_ref[...]
    dt = jnp.exp(ldt_ref[...])
    mag = jnp.exp(ar * dt)
    lr = mag * jnp.cos(ai * dt)
    li = mag * jnp.sin(ai * dt)
    den = ar * ar + ai * ai
    zr = ((lr - 1.0) * ar + li * ai) / den
    zi = (li * ar - (lr - 1.0) * ai) / den
    lr_ref[...] = lr
    li_ref[...] = li
    br = br_ref[...]
    bi = bi_ref[...]
    bbr_ref[...] = zr[:, None, :] * br - zi[:, None, :] * bi
    bbi_ref[...] = zr[:, None, :] * bi + zi[:, None, :] * br


def _ssmprep(a_re, a_im, log_dt, b_re_t, b_im_t):
    g, n = a_re.shape
    p = b_re_t.shape[1]
    return pl.pallas_call(
        _ssmprep_kernel,
        out_shape=[jax.ShapeDtypeStruct((g, n), F32), jax.ShapeDtypeStruct((g, n), F32),
                   jax.ShapeDtypeStruct((g, p, n), F32), jax.ShapeDtypeStruct((g, p, n), F32)],
        name="ssmprep",
    )(a_re, a_im, log_dt, b_re_t, b_im_t)


def _ssm_kernel(u_ref, bw_ref, cw_ref, lr_ref, li_ref, d_ref, gy_ref, xr_ref, xi_ref, hr_ref, hi_ref, *, tc, nb):
    rows = nb * tc
    tiles = tc // V7X_SUBLANES

    @pl.when(pl.program_id(0) == 0)
    def _():
        hr_ref[...] = jnp.zeros_like(hr_ref)
        hi_ref[...] = jnp.zeros_like(hi_ref)

    u = u_ref[...].reshape(rows, SSM_WIDTH)
    ub = u.astype(BF16)
    for cb in range(CH_BLOCKS):
        res = jnp.dot(ub[:, cb * V7X_LANES:(cb + 1) * V7X_LANES], bw_ref[cb], preferred_element_type=F32)
        for k in range(TILES_PER_BLOCK):
            st = (cb * TILES_PER_BLOCK + k) * V7X_SUBLANES
            xr_ref[:, st:st + V7X_SUBLANES, :] = res[:, k * V7X_LANES:(k + 1) * V7X_LANES].reshape(
                rows // V7X_SUBLANES, V7X_SUBLANES, V7X_LANES)
            xi_ref[:, st:st + V7X_SUBLANES, :] = res[:, STATES_PER_BLOCK + k * V7X_LANES:
                                                     STATES_PER_BLOCK + (k + 1) * V7X_LANES].reshape(
                rows // V7X_SUBLANES, V7X_SUBLANES, V7X_LANES)

    lr = lr_ref[...]
    li = li_ref[...]

    def tile_step(ti, carry):
        hs = list(carry)
        for s in range(V7X_SUBLANES):
            for b in range(nb):
                hr, hi = hs[2 * b], hs[2 * b + 1]
                sl = pl.ds(s, STATE_ROWS, stride=V7X_SUBLANES)
                xr = xr_ref[b * tiles + ti, sl, :]
                xi = xi_ref[b * tiles + ti, sl, :]
                nr = lr * hr - li * hi + xr
                ni = lr * hi + li * hr + xi
                xr_ref[b * tiles + ti, sl, :] = nr
                xi_ref[b * tiles + ti, sl, :] = ni
                hs[2 * b], hs[2 * b + 1] = nr, ni
        return tuple(hs)

    init = []
    for b in range(nb):
        init += [hr_ref[b], hi_ref[b]]
    fin = lax.fori_loop(0, tiles, tile_step, tuple(init))
    for b in range(nb):
        hr_ref[b] = fin[2 * b]
        hi_ref[b] = fin[2 * b + 1]

    for cb in range(CH_BLOCKS):
        parts = []
        for ref in (xr_ref, xi_ref):
            for k in range(TILES_PER_BLOCK):
                st = (cb * TILES_PER_BLOCK + k) * V7X_SUBLANES
                parts.append(ref[:, st:st + V7X_SUBLANES, :].reshape(rows, V7X_LANES).astype(BF16))
        hcat = jnp.concatenate(parts, axis=-1)
        y = jnp.dot(hcat, cw_ref[cb], preferred_element_type=F32)
        cols = slice(cb * V7X_LANES, (cb + 1) * V7X_LANES)
        y = y + d_ref[:, cols] * u[:, cols]
        gy_ref[:, :, cols] = _gelu_tanh(y).astype(BF16).reshape(nb, tc, V7X_LANES)


def _ssm(u3, bw, cw, lam_r, lam_i, dskip):
    nb, seq, _ = u3.shape
    tc = 128
    rows = nb * tc
    return pl.pallas_call(
        functools.partial(_ssm_kernel, tc=tc, nb=nb),
        grid=(seq // tc,),
        in_specs=[pl.BlockSpec((nb, tc, SSM_WIDTH), lambda j: (0, j, 0)),
                  pl.BlockSpec(bw.shape, lambda j: (0, 0, 0)),
                  pl.BlockSpec(cw.shape, lambda j: (0, 0, 0)),
                  pl.BlockSpec((STATE_ROWS, V7X_LANES), lambda j: (0, 0)),
                  pl.BlockSpec((STATE_ROWS, V7X_LANES), lambda j: (0, 0)),
                  pl.BlockSpec((1, SSM_WIDTH), lambda j: (0, 0))],
        out_specs=pl.BlockSpec((nb, tc, SSM_WIDTH), lambda j: (0, j, 0)),
        out_shape=jax.ShapeDtypeStruct((nb, seq, SSM_WIDTH), BF16),
        scratch_shapes=[pltpu.VMEM((rows // V7X_SUBLANES, STATE_ROWS * V7X_SUBLANES, V7X_LANES), F32),
                        pltpu.VMEM((rows // V7X_SUBLANES, STATE_ROWS * V7X_SUBLANES, V7X_LANES), F32),
                        pltpu.VMEM((nb, STATE_ROWS, V7X_LANES), F32),
                        pltpu.VMEM((nb, STATE_ROWS, V7X_LANES), F32)],
        compiler_params=_cparams(("arbitrary",), 40),
        name="ssm",
    )(u3, bw, cw, lam_r, lam_i, dskip)


def _mix_kernel(x_ref, mod_ref, g1n_ref, g2n_ref, o_ref, gy_ref, wga_ref, wgs_ref, wap_ref, wla_ref, wlb_ref,
                wout_ref, x1_ref, h2_ref, h_scr, acc_scr):
    n = pl.program_id(1)

    @pl.when(n == 0)
    def _():
        h_scr[...] = _modulated_norm(x_ref[...], g1n_ref[...], mod_ref[1:2, :], mod_ref[0:1, :]).astype(BF16)
        acc_scr[...] = jnp.zeros_like(acc_scr)

    h = h_scr[...]
    ga = jnp.dot(h, wga_ref[...], preferred_element_type=F32)
    gs = jnp.dot(h, wgs_ref[...], preferred_element_type=F32)
    at = jnp.dot(o_ref[...], wap_ref[...], preferred_element_type=F32)
    gy = gy_ref[...]
    la = jnp.dot(gy, wla_ref[...], preferred_element_type=F32)
    lb = jnp.dot(gy, wlb_ref[...], preferred_element_type=F32)
    ssm = la * _sigmoid(lb)
    mixed = _sigmoid(ga) * at + _sigmoid(gs) * ssm
    acc_scr[...] += jnp.dot(mixed.astype(BF16), wout_ref[...], preferred_element_type=F32)

    @pl.when(n == pl.num_programs(1) - 1)
    def _():
        x1 = x_ref[...] + mod_ref[2:3, :] * acc_scr[...]
        x1_ref[...] = x1
        h2_ref[...] = _modulated_norm(x1, g2n_ref[...], mod_ref[4:5, :], mod_ref[3:4, :]).astype(BF16)


def _mix(x2, mod, g1n, g2n, o, gy, wga, wgs, wap, wglu, wout, seq):
    t = x2.shape[0]
    tm, tn = 512, 512
    tpb = seq // tm
    nn = D_MODEL // tn
    return pl.pallas_call(
        _mix_kernel,
        grid=(t // tm, nn),
        in_specs=[pl.BlockSpec((tm, D_MODEL), lambda i, n: (i, 0)),
                  pl.BlockSpec((None, 6, D_MODEL), lambda i, n: (i // tpb, 0, 0)),
                  pl.BlockSpec((1, D_MODEL), lambda i, n: (0, 0)),
                  pl.BlockSpec((1, D_MODEL), lambda i, n: (0, 0)),
                  pl.BlockSpec((tm, ATTN_WIDTH), lambda i, n: (i, 0)),
                  pl.BlockSpec((tm, SSM_WIDTH), lambda i, n: (i, 0)),
                  pl.BlockSpec((D_MODEL, tn), lambda i, n: (0, n)),
                  pl.BlockSpec((D_MODEL, tn), lambda i, n: (0, n)),
                  pl.BlockSpec((ATTN_WIDTH, tn), lambda i, n: (0, n)),
                  pl.BlockSpec((SSM_WIDTH, tn), lambda i, n: (0, n)),
                  pl.BlockSpec((SSM_WIDTH, tn), lambda i, n: (0, nn + n)),
                  pl.BlockSpec((tn, D_MODEL), lambda i, n: (n, 0))],
        out_specs=[pl.BlockSpec((tm, D_MODEL), lambda i, n: (i, 0)),
                   pl.BlockSpec((tm, D_MODEL), lambda i, n: (i, 0))],
        out_shape=[jax.ShapeDtypeStruct((t, D_MODEL), F32),
                   jax.ShapeDtypeStruct((t, D_MODEL), BF16)],
        scratch_shapes=[pltpu.VMEM((tm, D_MODEL), BF16), pltpu.VMEM((tm, D_MODEL), F32)],
        compiler_params=_cparams(("arbitrary", "arbitrary"), 56),
        name="mix",
    )(x2, mod, g1n, g2n, o, gy, wga, wgs, wap, wglu, wglu, wout)


def _ffn_kernel(h_ref, halo_ref, x1_ref, mod_ref, wg_ref, wv_ref, cw_ref, cb_ref, wd_ref, gf_ref, out_ref,
                gate_scr, acc_scr, *, tm, tiles_per_seq, final_norm):
    f = pl.program_id(1)
    first = (pl.program_id(0) % tiles_per_seq) == 0

    @pl.when(f == 0)
    def _():
        acc_scr[...] = jnp.zeros_like(acc_scr)

    h = h_ref[...]
    wg = wg_ref[...]
    gate = jnp.dot(h, wg, preferred_element_type=F32)
    val = jnp.dot(h, wv_ref[...], preferred_element_type=F32)
    halo = jnp.dot(halo_ref[...], wg, preferred_element_type=F32)
    halo = jnp.where(first, 0.0, halo)
    gate_scr[0:V7X_SUBLANES, :] = halo
    gate_scr[V7X_SUBLANES:V7X_SUBLANES + tm, :] = gate
    g1 = gate_scr[V7X_SUBLANES - 1:V7X_SUBLANES - 1 + tm, :]
    g2 = gate_scr[V7X_SUBLANES - 2:V7X_SUBLANES - 2 + tm, :]
    conv = cw_ref[0:1, :] * g2 + cw_ref[1:2, :] * g1 + cw_ref[2:3, :] * gate + cb_ref[...]
    act = (_silu(conv) * val).astype(BF16)
    acc_scr[...] += jnp.dot(act, wd_ref[...], preferred_element_type=F32)

    @pl.when(f == pl.num_programs(1) - 1)
    def _():
        x2 = x1_ref[...] + mod_ref[5:6, :] * acc_scr[...]
        out_ref[...] = _rms(x2) * gf_ref[...] if final_norm else x2


def _ffn(h2, x1, mod, wup, conv_w, conv_b, wdown, gf, seq, final_norm):
    t = h2.shape[0]
    tm, tf = 512, 512
    tpb = seq // tm
    nf = D_FF // tf
    hb = tm // V7X_SUBLANES
    return pl.pallas_call(
        functools.partial(_ffn_kernel, tm=tm, tiles_per_seq=tpb, final_norm=final_norm),
        grid=(t // tm, nf),
        in_specs=[pl.BlockSpec((tm, D_MODEL), lambda i, f: (i, 0)),
                  pl.BlockSpec((V7X_SUBLANES, D_MODEL), lambda i, f: (jnp.maximum(i * hb - 1, 0), 0)),
                  pl.BlockSpec((tm, D_MODEL), lambda i, f: (i, 0)),
                  pl.BlockSpec((None, 6, D_MODEL), lambda i, f: (i // tpb, 0, 0)),
                  pl.BlockSpec((D_MODEL, tf), lambda i, f: (0, f)),
                  pl.BlockSpec((D_MODEL, tf), lambda i, f: (0, nf + f)),
                  pl.BlockSpec((3, tf), lambda i, f: (0, f)),
                  pl.BlockSpec((1, tf), lambda i, f: (0, f)),
                  pl.BlockSpec((tf, D_MODEL), lambda i, f: (f, 0)),
                  pl.BlockSpec((1, D_MODEL), lambda i, f: (0, 0))],
        out_specs=pl.BlockSpec((tm, D_MODEL), lambda i, f: (i, 0)),
        out_shape=jax.ShapeDtypeStruct((t, D_MODEL), F32),
        scratch_shapes=[pltpu.VMEM((tm + V7X_SUBLANES, tf), F32), pltpu.VMEM((tm, D_MODEL), F32)],
        compiler_params=_cparams(("arbitrary", "arbitrary"), 56),
        name="ffn",
    )(h2, h2, x1, mod, wup, wup, conv_w, conv_b, wdown, gf)


def _block_diag(w):
    cbs, gl, a, b = w.shape
    eye = jnp.eye(gl, dtype=w.dtype)
    return (w[:, :, :, None, :] * eye[None, :, None, :, None]).reshape(cbs, gl * a, gl * b)


def kernel(x, c, ada_w, ada_b, norm_mix_g, w_in, attn_sinks, w_attn_proj, ssm_a_re, ssm_a_im, ssm_log_dt,
           ssm_b_re, ssm_b_im, ssm_c_re, ssm_c_im, ssm_d, w_ssm_glu, w_out, norm_ffn_g, w_ffn_up, ffn_conv_w,
           ffn_conv_b, w_ffn_down, final_g):
    bsz, seq, _ = x.shape
    x2 = x.reshape(bsz * seq, D_MODEL)
    c_pad = jnp.pad(c, ((0, V7X_SUBLANES - bsz), (0, 0)))
    for i in range(ada_w.shape[0]):
        mod = _ada(c_pad, ada_w[i], ada_b[i].reshape(1, -1))[:bsz].reshape(bsz, 6, D_MODEL)

        win = w_in[i]
        w_qkvu = win[:, :QKV_WIDTH + SSM_WIDTH].astype(BF16)
        w_ga = win[:, QKV_WIDTH + SSM_WIDTH:QKV_WIDTH + SSM_WIDTH + D_MODEL].astype(BF16)
        w_gs = win[:, QKV_WIDTH + SSM_WIDTH + D_MODEL:].astype(BF16)
        g1n = norm_mix_g[i].reshape(1, D_MODEL)
        g2n = norm_ffn_g[i].reshape(1, D_MODEL)

        qkv, u = _inproj(x2, mod, g1n, w_qkvu, seq)
        o = _attention(attn_sinks[i], qkv, seq)

        lam_r, lam_i, bbar_r, bbar_i = _ssmprep(
            ssm_a_re[i], ssm_a_im[i], ssm_log_dt[i].reshape(-1, 1),
            ssm_b_re[i].transpose(0, 2, 1), ssm_b_im[i].transpose(0, 2, 1))
        shp = (CH_BLOCKS, GROUPS_PER_BLOCK, SSM_GROUP, SSM_STATE)
        bw = jnp.concatenate([_block_diag(bbar_r.reshape(shp)), _block_diag(bbar_i.reshape(shp))],
                             axis=-1).astype(BF16)
        c_r = ssm_c_re[i].reshape(shp).transpose(0, 1, 3, 2)
        c_i = ssm_c_im[i].reshape(shp).transpose(0, 1, 3, 2)
        cw = jnp.concatenate([_block_diag(c_r), -_block_diag(c_i)], axis=1).astype(BF16)
        gy = _ssm(u.reshape(bsz, seq, SSM_WIDTH), bw, cw,
                  lam_r.reshape(STATE_ROWS, V7X_LANES), lam_i.reshape(STATE_ROWS, V7X_LANES),
                  ssm_d[i].reshape(1, SSM_WIDTH))

        x2, h2 = _mix(x2, mod, g1n, g2n, o, gy.reshape(bsz * seq, SSM_WIDTH), w_ga, w_gs,
                      w_attn_proj[i].astype(BF16), w_ssm_glu[i].astype(BF16), w_out[i].astype(BF16), seq)
        x2 = _ffn(h2, x2, mod, w_ffn_up[i].astype(BF16), ffn_conv_w[i], ffn_conv_b[i].reshape(1, D_FF),
                  w_ffn_down[i].astype(BF16), final_g.reshape(1, D_MODEL), seq, i == ada_w.shape[0] - 1)
    return x2.reshape(bsz, seq, D_MODEL)
```

```python
import functools

import jax
import jax.numpy as jnp
import numpy as np
from jax import lax
from jax.experimental import pallas as pl
from jax.experimental.pallas import tpu as pltpu

F32 = jnp.float32
BF16 = jnp.bfloat16

D_MODEL = 2048
HEAD_DIM = 64
N_Q_HEADS = 16
N_KV_HEADS = 2
Q_PER_KV = N_Q_HEADS // N_KV_HEADS
ATTN_WIDTH = N_Q_HEADS * HEAD_DIM
KV_WIDTH = N_KV_HEADS * HEAD_DIM
QKV_WIDTH = ATTN_WIDTH + 2 * KV_WIDTH
WINDOW = 128
BLOCK = 128
NEG_INF = -1e30
SSM_WIDTH = D_MODEL // 2
SSM_GROUP = 16
N_SSM_GROUPS = SSM_WIDTH // SSM_GROUP
SSM_STATE = 64
N_STATES = N_SSM_GROUPS * SSM_STATE
D_FF = 5632
RMS_EPS = 1e-6

V7X_SUBLANES = 8
V7X_LANES = 128
STATE_ROWS = N_STATES // V7X_LANES
CH_BLOCKS = SSM_WIDTH // V7X_LANES
GROUPS_PER_BLOCK = V7X_LANES // SSM_GROUP
STATES_PER_BLOCK = GROUPS_PER_BLOCK * SSM_STATE
TILES_PER_BLOCK = STATES_PER_BLOCK // V7X_LANES


def _cparams(sem, vmem_mb):
    return pltpu.CompilerParams(dimension_semantics=sem, vmem_limit_bytes=vmem_mb * 1024 * 1024)


def _sigmoid(x):
    return 1.0 / (1.0 + jnp.exp(-x))


def _silu(x):
    return x * _sigmoid(x)


def _gelu_tanh(x):
    c = np.float32(np.sqrt(2.0 / np.pi))
    return x * (0.5 * (1.0 + jnp.tanh(c * (x + np.float32(0.044715) * (x * x * x)))))


def _rms(x):
    return x * lax.rsqrt(jnp.mean(x * x, axis=-1, keepdims=True) + RMS_EPS)


def _modulated_norm(x, g, scale, shift):
    return _rms(x) * g * (1.0 + scale) + shift


def _ada_kernel(c_ref, w_ref, b_ref, o_ref):
    cond = _silu(c_ref[...]).astype(BF16)
    o_ref[...] = jnp.dot(cond, w_ref[...].astype(BF16), preferred_element_type=F32) + b_ref[...]


def _ada(c_pad, w, b):
    n = w.shape[1]
    tn = 1024
    return pl.pallas_call(
        _ada_kernel,
        grid=(n // tn,),
        in_specs=[pl.BlockSpec((c_pad.shape[0], D_MODEL), lambda j: (0, 0)),
                  pl.BlockSpec((D_MODEL, tn), lambda j: (0, j)),
                  pl.BlockSpec((1, tn), lambda j: (0, j))],
        out_specs=pl.BlockSpec((c_pad.shape[0], tn), lambda j: (0, j)),
        out_shape=jax.ShapeDtypeStruct((c_pad.shape[0], n), F32),
        compiler_params=_cparams(("arbitrary",), 40),
        name="ada",
    )(c_pad, w, b)


def _inproj_kernel(x_ref, mod_ref, g_ref, w_ref, h_ref, qkv_ref, u_ref, wb_scr):
    @pl.when(pl.program_id(0) == 0)
    def _():
        wb_scr[...] = w_ref[...].astype(BF16)

    h = _modulated_norm(x_ref[...], g_ref[...], mod_ref[1:2, :], mod_ref[0:1, :]).astype(BF16)
    h_ref[...] = h
    p = jnp.dot(h, wb_scr[...], preferred_element_type=F32)
    qkv_ref[...] = p[:, :QKV_WIDTH].astype(BF16)
    u_ref[...] = p[:, QKV_WIDTH:]


def _inproj(x2, mod, g, w_in, seq):
    t = x2.shape[0]
    tm = 512
    tpb = seq // tm
    n = QKV_WIDTH + SSM_WIDTH
    return pl.pallas_call(
        _inproj_kernel,
        grid=(t // tm,),
        in_specs=[pl.BlockSpec((tm, D_MODEL), lambda i: (i, 0)),
                  pl.BlockSpec((None, 6, D_MODEL), lambda i: (i // tpb, 0, 0)),
                  pl.BlockSpec((1, D_MODEL), lambda i: (0, 0)),
                  pl.BlockSpec((D_MODEL, n), lambda i: (0, 0), pipeline_mode=pl.Buffered(1))],
        out_specs=[pl.BlockSpec((tm, D_MODEL), lambda i: (i, 0)),
                   pl.BlockSpec((tm, QKV_WIDTH), lambda i: (i, 0)),
                   pl.BlockSpec((tm, SSM_WIDTH), lambda i: (i, 0))],
        out_shape=[jax.ShapeDtypeStruct((t, D_MODEL), BF16),
                   jax.ShapeDtypeStruct((t, QKV_WIDTH), BF16),
                   jax.ShapeDtypeStruct((t, SSM_WIDTH), F32)],
        scratch_shapes=[pltpu.VMEM((D_MODEL, n), BF16)],
        compiler_params=_cparams(("arbitrary",), 56),
        name="inproj",
    )(x2, mod, g, w_in)


def _attn_kernel(sink_ref, q_ref, kvc_ref, kvp_ref, o_ref, s_scr, p_scr, t_scr, *, tq, tiles_per_seq):
    first = (pl.program_id(0) % tiles_per_seq) == 0
    qi = lax.broadcasted_iota(jnp.int32, (BLOCK, BLOCK), 0)
    kj = lax.broadcasted_iota(jnp.int32, (BLOCK, BLOCK), 1)
    upper = kj > qi
    pairs = Q_PER_KV // 2
    pair_w = 2 * HEAD_DIM
    zeros = jnp.zeros((2 * BLOCK, HEAD_DIM), BF16)
    ones = jnp.ones((2 * BLOCK, pair_w), BF16)
    scale = jnp.asarray(HEAD_DIM ** -0.5, BF16)
    contract_lanes = (((1,), (1,)), ((), ()))
    n_qb = tq // BLOCK

    def window(qb):
        r0 = qb * BLOCK
        prev = kvp_ref[...] if qb == 0 else kvc_ref[r0 - BLOCK:r0, :]
        return jnp.concatenate([prev, kvc_ref[r0:r0 + BLOCK, :]], axis=0)

    def slot(qb, g, par):
        return (qb * N_KV_HEADS + g) * 2 + par

    for qb in range(n_qb):
        r0 = qb * BLOCK
        win = window(qb)
        for g in range(N_KV_HEADS):
            kg = win[:, g * HEAD_DIM:(g + 1) * HEAD_DIM]
            kz = (jnp.concatenate([kg, zeros], axis=1), jnp.concatenate([zeros, kg], axis=1))
            c0 = g * Q_PER_KV * HEAD_DIM
            qg = q_ref[r0:r0 + BLOCK, c0:c0 + Q_PER_KV * HEAD_DIM] * scale
            qs = jnp.concatenate([qg[:, p * pair_w:(p + 1) * pair_w] for p in range(pairs)], axis=0)
            for par in range(2):
                s_scr[slot(qb, g, par)] = lax.dot_general(qs, kz[par], contract_lanes, preferred_element_type=F32)

    for qb in range(n_qb):
        pad_bias = jnp.where(first, NEG_INF, 0.0).astype(F32) if qb == 0 else None
        for g in range(N_KV_HEADS):
            for par in range(2):
                sl = slot(qb, g, par)
                for p in range(pairs):
                    rows = slice(p * BLOCK, (p + 1) * BLOCK)
                    s_prev = s_scr[sl, rows, :BLOCK]
                    if pad_bias is not None:
                        s_prev = s_prev + pad_bias
                    sm = jnp.where(upper, s_prev, s_scr[sl, rows, BLOCK:])
                    sink = sink_ref[g * Q_PER_KV + 2 * p + par]
                    m = jnp.maximum(jnp.max(sm, axis=-1, keepdims=True), sink)
                    m = jnp.broadcast_to(m, (BLOCK, BLOCK))
                    pe = jnp.exp(sm - m)
                    p_scr[sl, rows, :BLOCK] = jnp.where(upper, pe, 0.0).astype(BF16)
                    p_scr[sl, rows, BLOCK:] = jnp.where(upper, 0.0, pe).astype(BF16)
                    t_scr[sl, rows, :] = jnp.exp(sink - m)

    for qb in range(n_qb):
        r0 = qb * BLOCK
        win = window(qb)
        for g in range(N_KV_HEADS):
            vg = win[:, KV_WIDTH + g * HEAD_DIM:KV_WIDTH + (g + 1) * HEAD_DIM]
            vz = (jnp.concatenate([vg, zeros, ones], axis=1), jnp.concatenate([zeros, vg, ones], axis=1))
            c0 = g * Q_PER_KV * HEAD_DIM
            acc = None
            for par in range(2):
                sl = slot(qb, g, par)
                res = jnp.dot(p_scr[sl], vz[par], preferred_element_type=F32)
                on = res[:, :pair_w] / (res[:, pair_w:] + t_scr[sl])
                acc = on if acc is None else acc + on
            for p in range(pairs):
                o_ref[r0:r0 + BLOCK, c0 + p * pair_w:c0 + (p + 1) * pair_w] = (
                    acc[p * BLOCK:(p + 1) * BLOCK].astype(BF16))


def _attention(sinks, qkv, seq):
    t = qkv.shape[0]
    tq = 512
    tiles_per_seq = seq // tq
    kv_col = ATTN_WIDTH // (2 * KV_WIDTH)
    rb = tq // BLOCK
    slots = rb * N_KV_HEADS * 2
    rows = (Q_PER_KV // 2) * BLOCK
    return pl.pallas_call(
        functools.partial(_attn_kernel, tq=tq, tiles_per_seq=tiles_per_seq),
        grid=(t // tq,),
        in_specs=[pl.BlockSpec(memory_space=pltpu.SMEM),
                  pl.BlockSpec((tq, ATTN_WIDTH), lambda i: (i, 0)),
                  pl.BlockSpec((tq, 2 * KV_WIDTH), lambda i: (i, kv_col)),
                  pl.BlockSpec((BLOCK, 2 * KV_WIDTH), lambda i: (jnp.maximum(i * rb - 1, 0), kv_col))],
        out_specs=pl.BlockSpec((tq, ATTN_WIDTH), lambda i: (i, 0)),
        out_shape=jax.ShapeDtypeStruct((t, ATTN_WIDTH), BF16),
        scratch_shapes=[pltpu.VMEM((slots, rows, 2 * BLOCK), F32),
                        pltpu.VMEM((slots, rows, 2 * BLOCK), BF16),
                        pltpu.VMEM((slots, rows, BLOCK), F32)],
        compiler_params=_cparams(("arbitrary",), 40),
        name="attn",
    )(sinks, qkv, qkv, qkv)


def _ssmprep_kernel(ar_ref, ai_ref, ldt_ref, br_ref, bi_ref, lr_ref, li_ref, bbr_ref, bbi_ref):
    ar = ar_ref[...]
    ai = ai_ref[...]
    dt = jnp.exp(ldt_ref[...])
    mag = jnp.exp(ar * dt)
    lr = mag * jnp.cos(ai * dt)
    li = mag * jnp.sin(ai * dt)
    den = ar * ar + ai * ai
    zr = ((lr - 1.0) * ar + li * ai) / den
    zi = (li * ar - (lr - 1.0) * ai) / den
    lr_ref[...] = lr
    li_ref[...] = li
    br = br_ref[...]
    bi = bi_ref[...]
    bbr_ref[...] = zr[:, None, :] * br - zi[:, None, :] * bi
    bbi_ref[...] = zr[:, None, :] * bi + zi[:, None, :] * br


def _ssmprep(a_re, a_im, log_dt, b_re_t, b_im_t):
    g, n = a_re.shape
    p = b_re_t.shape[1]
    return pl.pallas_call(
        _ssmprep_kernel,
        out_shape=[jax.ShapeDtypeStruct((g, n), F32), jax.ShapeDtypeStruct((g, n), F32),
                   jax.ShapeDtypeStruct((g, p, n), F32), jax.ShapeDtypeStruct((g, p, n), F32)],
        name="ssmprep",
    )(a_re, a_im, log_dt, b_re_t, b_im_t)


SLAB_PITCH = 2 * STATE_ROWS + 4


def _ssm_kernel(u_ref, bw_ref, cw_ref, lr_ref, li_ref, d_ref, gy_ref, hcr_ref, hci_ref, *bufs, tc, nb):
    tiles = tc // V7X_SUBLANES

    @pl.when(pl.program_id(0) == 0)
    def _():
        hcr_ref[...] = jnp.zeros_like(hcr_ref)
        hci_ref[...] = jnp.zeros_like(hci_ref)

    def tile_rows(ti, state_tile, imag):
        return pl.ds(ti * V7X_SUBLANES * SLAB_PITCH + 2 * state_tile + imag, V7X_SUBLANES, stride=SLAB_PITCH)

    def slab_rows(t, imag):
        return pl.ds(t * SLAB_PITCH + imag, STATE_ROWS, stride=2)

    def project_in(b):
        x_ref = bufs[2 * b]
        ub = u_ref[b].astype(BF16)
        for cb in range(CH_BLOCKS):
            res = jnp.dot(ub[:, cb * V7X_LANES:(cb + 1) * V7X_LANES], bw_ref[cb], preferred_element_type=F32)
            for k in range(TILES_PER_BLOCK):
                for ti in range(tiles):
                    rs = slice(ti * V7X_SUBLANES, (ti + 1) * V7X_SUBLANES)
                    st = cb * TILES_PER_BLOCK + k
                    x_ref[tile_rows(ti, st, 0), :] = res[rs, k * V7X_LANES:(k + 1) * V7X_LANES]
                    x_ref[tile_rows(ti, st, 1), :] = res[rs, STATES_PER_BLOCK + k * V7X_LANES:
                                                         STATES_PER_BLOCK + (k + 1) * V7X_LANES]

    def recur(b):
        x_ref, h_ref = bufs[2 * b], bufs[2 * b + 1]
        lr = lr_ref[...]
        li = li_ref[...]
        hr = hcr_ref[b]
        hi = hci_ref[b]
        for t in range(tc):
            nr = lr * hr - li * hi + x_ref[slab_rows(t, 0), :]
            ni = lr * hi + li * hr + x_ref[slab_rows(t, 1), :]
            h_ref[slab_rows(t, 0), :] = nr
            h_ref[slab_rows(t, 1), :] = ni
            hr, hi = nr, ni
        hcr_ref[b] = hr
        hci_ref[b] = hi

    def project_out(b):
        h_ref = bufs[2 * b + 1]
        u = u_ref[b]
        for cb in range(CH_BLOCKS):
            row_tiles = []
            for ti in range(tiles):
                parts = [h_ref[tile_rows(ti, cb * TILES_PER_BLOCK + k, imag), :]
                         for imag in (0, 1) for k in range(TILES_PER_BLOCK)]
                row_tiles.append(jnp.concatenate(parts, axis=-1))
            hcat = jnp.concatenate(row_tiles, axis=0).astype(BF16)
            y = jnp.dot(hcat, cw_ref[cb], preferred_element_type=F32)
            cols = slice(cb * V7X_LANES, (cb + 1) * V7X_LANES)
            y = y + d_ref[:, cols] * u[:, cols]
            gy_ref[b, :, cols] = _gelu_tanh(y).astype(BF16)

    for b in range(nb):
        project_in(b)
    for b in range(nb):
        recur(b)
        project_out(b)


def _ssm(u3, bw, cw, lam_r, lam_i, dskip):
    nb, seq, _ = u3.shape
    tc = 128
    slab_buf = pltpu.VMEM((tc * SLAB_PITCH, V7X_LANES), F32)
    return pl.pallas_call(
        functools.partial(_ssm_kernel, tc=tc, nb=nb),
        grid=(seq // tc,),
        in_specs=[pl.BlockSpec((nb, tc, SSM_WIDTH), lambda j: (0, j, 0)),
                  pl.BlockSpec(bw.shape, lambda j: (0, 0, 0)),
                  pl.BlockSpec(cw.shape, lambda j: (0, 0, 0)),
                  pl.BlockSpec((STATE_ROWS, V7X_LANES), lambda j: (0, 0)),
                  pl.BlockSpec((STATE_ROWS, V7X_LANES), lambda j: (0, 0)),
                  pl.BlockSpec((1, SSM_WIDTH), lambda j: (0, 0))],
        out_specs=pl.BlockSpec((nb, tc, SSM_WIDTH), lambda j: (0, j, 0)),
        out_shape=jax.ShapeDtypeStruct((nb, seq, SSM_WIDTH), BF16),
        scratch_shapes=[pltpu.VMEM((nb, STATE_ROWS, V7X_LANES), F32),
                        pltpu.VMEM((nb, STATE_ROWS, V7X_LANES), F32)] + [slab_buf] * (2 * nb),
        compiler_params=_cparams(("arbitrary",), 48),
        name="ssm",
    )(u3, bw, cw, lam_r, lam_i, dskip)


def _mix_kernel(x_ref, h_ref, mod_ref, g2n_ref, o_ref, gy_ref, wga_ref, wgs_ref, wap_ref, wla_ref, wlb_ref,
                wout_ref, x1_ref, h2_ref):
    n = pl.program_id(1)

    @pl.when(n == 0)
    def _():
        x1_ref[...] = jnp.zeros_like(x1_ref)

    h = h_ref[...]
    ga = jnp.dot(h, wga_ref[...], preferred_element_type=F32)
    gs = jnp.dot(h, wgs_ref[...], preferred_element_type=F32)
    at = jnp.dot(o_ref[...], wap_ref[...], preferred_element_type=F32)
    gy = gy_ref[...]
    la = jnp.dot(gy, wla_ref[...], preferred_element_type=F32)
    lb = jnp.dot(gy, wlb_ref[...], preferred_element_type=F32)
    ssm = la * _sigmoid(lb)
    mixed = _sigmoid(ga) * at + _sigmoid(gs) * ssm
    x1_ref[...] += jnp.dot(mixed.astype(BF16), wout_ref[...], preferred_element_type=F32)

    @pl.when(n == pl.num_programs(1) - 1)
    def _():
        x1 = x_ref[...] + mod_ref[2:3, :] * x1_ref[...]
        x1_ref[...] = x1
        h2_ref[...] = _modulated_norm(x1, g2n_ref[...], mod_ref[4:5, :], mod_ref[3:4, :]).astype(BF16)


def _mix(x2, h, mod, g2n, o, gy3, wga, wgs, wap, wglu, wout, seq):
    t = x2.shape[0]
    tm, tn = 512, 512
    tpb = seq // tm
    nn = D_MODEL // tn
    return pl.pallas_call(
        _mix_kernel,
        grid=(t // tm, nn),
        in_specs=[pl.BlockSpec((tm, D_MODEL), lambda i, n: (i, 0)),
                  pl.BlockSpec((tm, D_MODEL), lambda i, n: (i, 0)),
                  pl.BlockSpec((None, 6, D_MODEL), lambda i, n: (i // tpb, 0, 0)),
                  pl.BlockSpec((1, D_MODEL), lambda i, n: (0, 0)),
                  pl.BlockSpec((tm, ATTN_WIDTH), lambda i, n: (i, 0)),
                  pl.BlockSpec((None, tm, SSM_WIDTH), lambda i, n: (i // tpb, i % tpb, 0)),
                  pl.BlockSpec((D_MODEL, tn), lambda i, n: (0, n)),
                  pl.BlockSpec((D_MODEL, tn), lambda i, n: (0, n)),
                  pl.BlockSpec((ATTN_WIDTH, tn), lambda i, n: (0, n)),
                  pl.BlockSpec((SSM_WIDTH, tn), lambda i, n: (0, n)),
                  pl.BlockSpec((SSM_WIDTH, tn), lambda i, n: (0, nn + n)),
                  pl.BlockSpec((tn, D_MODEL), lambda i, n: (n, 0))],
        out_specs=[pl.BlockSpec((tm, D_MODEL), lambda i, n: (i, 0)),
                   pl.BlockSpec((tm, D_MODEL), lambda i, n: (i, 0))],
        out_shape=[jax.ShapeDtypeStruct((t, D_MODEL), F32),
                   jax.ShapeDtypeStruct((t, D_MODEL), BF16)],
        compiler_params=_cparams(("arbitrary", "arbitrary"), 56),
        name="mix",
    )(x2, h, mod, g2n, o, gy3, wga, wgs, wap, wglu, wglu, wout)


def _ffn_kernel(h_ref, halo_ref, x1_ref, mod_ref, wg_ref, wv_ref, cw_ref, cb_ref, wd_ref, gf_ref, out_ref,
                gate_scr, *, tm, tiles_per_seq, final_norm):
    f = pl.program_id(1)
    first = (pl.program_id(0) % tiles_per_seq) == 0

    @pl.when(f == 0)
    def _():
        out_ref[...] = jnp.zeros_like(out_ref)

    h = h_ref[...]
    wg = wg_ref[...].astype(BF16)
    gate = jnp.dot(h, wg, preferred_element_type=F32)
    val = jnp.dot(h, wv_ref[...].astype(BF16), preferred_element_type=F32)
    halo = jnp.dot(halo_ref[...], wg, preferred_element_type=F32)
    halo = jnp.where(first, 0.0, halo)
    gate_scr[0:V7X_SUBLANES, :] = halo
    gate_scr[V7X_SUBLANES:V7X_SUBLANES + tm, :] = gate
    g1 = gate_scr[V7X_SUBLANES - 1:V7X_SUBLANES - 1 + tm, :]
    g2 = gate_scr[V7X_SUBLANES - 2:V7X_SUBLANES - 2 + tm, :]
    conv = cw_ref[0:1, :] * g2 + cw_ref[1:2, :] * g1 + cw_ref[2:3, :] * gate + cb_ref[...]
    act = (_silu(conv) * val).astype(BF16)
    out_ref[...] += jnp.dot(act, wd_ref[...].astype(BF16), preferred_element_type=F32)

    @pl.when(f == pl.num_programs(1) - 1)
    def _():
        x2 = x1_ref[...] + mod_ref[5:6, :] * out_ref[...]
        out_ref[...] = _rms(x2) * gf_ref[...] if final_norm else x2


def _ffn(h2, x1, mod, wup, conv_w, conv_b, wdown, gf, seq, final_norm):
    t = h2.shape[0]
    tm, tf = 1024, 256
    tpb = seq // tm
    nf = D_FF // tf
    hb = tm // V7X_SUBLANES
    return pl.pallas_call(
        functools.partial(_ffn_kernel, tm=tm, tiles_per_seq=tpb, final_norm=final_norm),
        grid=(t // tm, nf),
        in_specs=[pl.BlockSpec((tm, D_MODEL), lambda i, f: (i, 0)),
                  pl.BlockSpec((V7X_SUBLANES, D_MODEL), lambda i, f: (jnp.maximum(i * hb - 1, 0), 0)),
                  pl.BlockSpec((tm, D_MODEL), lambda i, f: (i, 0), pipeline_mode=pl.Buffered(1)),
                  pl.BlockSpec((None, 6, D_MODEL), lambda i, f: (i // tpb, 0, 0)),
                  pl.BlockSpec((D_MODEL, tf), lambda i, f: (0, f)),
                  pl.BlockSpec((D_MODEL, tf), lambda i, f: (0, nf + f)),
                  pl.BlockSpec((3, tf), lambda i, f: (0, f)),
                  pl.BlockSpec((1, tf), lambda i, f: (0, f)),
                  pl.BlockSpec((tf, D_MODEL), lambda i, f: (f, 0)),
                  pl.BlockSpec((1, D_MODEL), lambda i, f: (0, 0))],
        out_specs=pl.BlockSpec((tm, D_MODEL), lambda i, f: (i, 0)),
        out_shape=jax.ShapeDtypeStruct((t, D_MODEL), F32),
        scratch_shapes=[pltpu.VMEM((tm + V7X_SUBLANES, tf), F32)],
        compiler_params=_cparams(("arbitrary", "arbitrary"), 60),
        name="ffn",
    )(h2, h2, x1, mod, wup, wup, conv_w, conv_b, wdown, gf)


def _block_diag(w):
    cbs, gl, a, b = w.shape
    eye = jnp.eye(gl, dtype=w.dtype)
    return (w[:, :, :, None, :] * eye[None, :, None, :, None]).reshape(cbs, gl * a, gl * b)


def kernel(x, c, ada_w, ada_b, norm_mix_g, w_in, attn_sinks, w_attn_proj, ssm_a_re, ssm_a_im, ssm_log_dt,
           ssm_b_re, ssm_b_im, ssm_c_re, ssm_c_im, ssm_d, w_ssm_glu, w_out, norm_ffn_g, w_ffn_up, ffn_conv_w,
           ffn_conv_b, w_ffn_down, final_g):
    bsz, seq, _ = x.shape
    x2 = x.reshape(bsz * seq, D_MODEL)
    c_pad = jnp.pad(c, ((0, V7X_SUBLANES - bsz), (0, 0)))
    for i in range(ada_w.shape[0]):
        mod = _ada(c_pad, ada_w[i], ada_b[i].reshape(1, -1))[:bsz].reshape(bsz, 6, D_MODEL)

        win = w_in[i]
        w_ga = win[:, QKV_WIDTH + SSM_WIDTH:QKV_WIDTH + SSM_WIDTH + D_MODEL].astype(BF16)
        w_gs = win[:, QKV_WIDTH + SSM_WIDTH + D_MODEL:].astype(BF16)
        g1n = norm_mix_g[i].reshape(1, D_MODEL)
        g2n = norm_ffn_g[i].reshape(1, D_MODEL)

        h, qkv, u = _inproj(x2, mod, g1n, win, seq)
        o = _attention(attn_sinks[i], qkv, seq)

        lam_r, lam_i, bbar_r, bbar_i = _ssmprep(
            ssm_a_re[i], ssm_a_im[i], ssm_log_dt[i].reshape(-1, 1),
            ssm_b_re[i].transpose(0, 2, 1), ssm_b_im[i].transpose(0, 2, 1))
        shp = (CH_BLOCKS, GROUPS_PER_BLOCK, SSM_GROUP, SSM_STATE)
        bw = jnp.concatenate([_block_diag(bbar_r.reshape(shp)), _block_diag(bbar_i.reshape(shp))],
                             axis=-1).astype(BF16)
        c_r = ssm_c_re[i].reshape(shp).transpose(0, 1, 3, 2)
        c_i = ssm_c_im[i].reshape(shp).transpose(0, 1, 3, 2)
        cw = jnp.concatenate([_block_diag(c_r), -_block_diag(c_i)], axis=1).astype(BF16)
        gy = _ssm(u.reshape(bsz, seq, SSM_WIDTH), bw, cw,
                  lam_r.reshape(STATE_ROWS, V7X_LANES), lam_i.reshape(STATE_ROWS, V7X_LANES),
                  ssm_d[i].reshape(1, SSM_WIDTH))

        x2, h2 = _mix(x2, h, mod, g2n, o, gy, w_ga, w_gs,
                      w_attn_proj[i].astype(BF16), w_ssm_glu[i].astype(BF16), w_out[i].astype(BF16), seq)
        x2 = _ffn(h2, x2, mod, w_ffn_up[i], ffn_conv_w[i], ffn_conv_b[i].reshape(1, D_FF),
                  w_ffn_down[i], final_g.reshape(1, D_MODEL), seq, i == ada_w.shape[0] - 1)
    return x2.reshape(bsz, seq, D_MODEL)
```

```python
import functools

import jax
import jax.numpy as jnp
import numpy as np
from jax import lax
from jax.experimental import pallas as pl
from jax.experimental.pallas import tpu as pltpu

F32 = jnp.float32
BF16 = jnp.bfloat16

D_MODEL = 2048
HEAD_DIM = 64
N_Q_HEADS = 16
N_KV_HEADS = 2
Q_PER_KV = N_Q_HEADS // N_KV_HEADS
ATTN_WIDTH = N_Q_HEADS * HEAD_DIM
KV_WIDTH = N_KV_HEADS * HEAD_DIM
QKV_WIDTH = ATTN_WIDTH + 2 * KV_WIDTH
WINDOW = 128
BLOCK = 128
NEG_INF = -1e30
SSM_WIDTH = D_MODEL // 2
SSM_GROUP = 16
N_SSM_GROUPS = SSM_WIDTH // SSM_GROUP
SSM_STATE = 64
N_STATES = N_SSM_GROUPS * SSM_STATE
D_FF = 5632
RMS_EPS = 1e-6

V7X_SUBLANES = 8
V7X_LANES = 128
STATE_ROWS = N_STATES // V7X_LANES
CH_BLOCKS = SSM_WIDTH // V7X_LANES
GROUPS_PER_BLOCK = V7X_LANES // SSM_GROUP
STATES_PER_BLOCK = GROUPS_PER_BLOCK * SSM_STATE
TILES_PER_BLOCK = STATES_PER_BLOCK // V7X_LANES


def _cparams(sem, vmem_mb):
    return pltpu.CompilerParams(dimension_semantics=sem, vmem_limit_bytes=vmem_mb * 1024 * 1024)


def _sigmoid(x):
    return 1.0 / (1.0 + jnp.exp(-x))


def _silu(x):
    return x * _sigmoid(x)


def _gelu_tanh(x):
    c = np.float32(np.sqrt(2.0 / np.pi))
    return x * (0.5 * (1.0 + jnp.tanh(c * (x + np.float32(0.044715) * (x * x * x)))))


def _rms(x):
    return x * lax.rsqrt(jnp.mean(x * x, axis=-1, keepdims=True) + RMS_EPS)


def _modulated_norm(x, g, scale, shift):
    return _rms(x) * g * (1.0 + scale) + shift


def _ada_kernel(c_ref, w_ref, b_ref, o_ref):
    cond = _silu(c_ref[...]).astype(BF16)
    o_ref[...] = jnp.dot(cond, w_ref[...].astype(BF16), preferred_element_type=F32) + b_ref[...]


def _ada(c_pad, w, b):
    n = w.shape[1]
    tn = 1024
    return pl.pallas_call(
        _ada_kernel,
        grid=(n // tn,),
        in_specs=[pl.BlockSpec((c_pad.shape[0], D_MODEL), lambda j: (0, 0)),
                  pl.BlockSpec((D_MODEL, tn), lambda j: (0, j)),
                  pl.BlockSpec((1, tn), lambda j: (0, j))],
        out_specs=pl.BlockSpec((c_pad.shape[0], tn), lambda j: (0, j)),
        out_shape=jax.ShapeDtypeStruct((c_pad.shape[0], n), F32),
        compiler_params=_cparams(("arbitrary",), 40),
        name="ada",
    )(c_pad, w, b)


def _inproj_kernel(x_ref, mod_ref, g_ref, w_ref, h_ref, qkv_ref, u_ref, wb_scr):
    @pl.when(pl.program_id(0) == 0)
    def _():
        wb_scr[...] = w_ref[...].astype(BF16)

    h = _modulated_norm(x_ref[...], g_ref[...], mod_ref[1:2, :], mod_ref[0:1, :]).astype(BF16)
    h_ref[...] = h
    p = jnp.dot(h, wb_scr[...], preferred_element_type=F32)
    qkv_ref[...] = p[:, :QKV_WIDTH].astype(BF16)
    u_ref[...] = p[:, QKV_WIDTH:]


def _inproj(x2, mod, g, w_in, seq):
    t = x2.shape[0]
    tm = 512
    tpb = seq // tm
    n = QKV_WIDTH + SSM_WIDTH
    return pl.pallas_call(
        _inproj_kernel,
        grid=(t // tm,),
        in_specs=[pl.BlockSpec((tm, D_MODEL), lambda i: (i, 0)),
                  pl.BlockSpec((None, 6, D_MODEL), lambda i: (i // tpb, 0, 0)),
                  pl.BlockSpec((1, D_MODEL), lambda i: (0, 0)),
                  pl.BlockSpec((D_MODEL, n), lambda i: (0, 0), pipeline_mode=pl.Buffered(1))],
        out_specs=[pl.BlockSpec((tm, D_MODEL), lambda i: (i, 0)),
                   pl.BlockSpec((tm, QKV_WIDTH), lambda i: (i, 0)),
                   pl.BlockSpec((tm, SSM_WIDTH), lambda i: (i, 0))],
        out_shape=[jax.ShapeDtypeStruct((t, D_MODEL), BF16),
                   jax.ShapeDtypeStruct((t, QKV_WIDTH), BF16),
                   jax.ShapeDtypeStruct((t, SSM_WIDTH), F32)],
        scratch_shapes=[pltpu.VMEM((D_MODEL, n), BF16)],
        compiler_params=_cparams(("arbitrary",), 56),
        name="inproj",
    )(x2, mod, g, w_in)


def _attn_kernel(sink_ref, q_ref, kvc_ref, kvp_ref, o_ref, s_scr, p_scr, t_scr, *, tq, tiles_per_seq):
    first = (pl.program_id(0) % tiles_per_seq) == 0
    qi = lax.broadcasted_iota(jnp.int32, (BLOCK, BLOCK), 0)
    kj = lax.broadcasted_iota(jnp.int32, (BLOCK, BLOCK), 1)
    upper = kj > qi
    pairs = Q_PER_KV // 2
    pair_w = 2 * HEAD_DIM
    zeros = jnp.zeros((2 * BLOCK, HEAD_DIM), BF16)
    ones = jnp.ones((2 * BLOCK, pair_w), BF16)
    scale = jnp.asarray(HEAD_DIM ** -0.5, BF16)
    contract_lanes = (((1,), (1,)), ((), ()))
    n_qb = tq // BLOCK

    def window(qb):
        r0 = qb * BLOCK
        prev = kvp_ref[...] if qb == 0 else kvc_ref[r0 - BLOCK:r0, :]
        return jnp.concatenate([prev, kvc_ref[r0:r0 + BLOCK, :]], axis=0)

    def slot(qb, g, par):
        return (qb * N_KV_HEADS + g) * 2 + par

    for qb in range(n_qb):
        r0 = qb * BLOCK
        win = window(qb)
        for g in range(N_KV_HEADS):
            kg = win[:, g * HEAD_DIM:(g + 1) * HEAD_DIM]
            kz = (jnp.concatenate([kg, zeros], axis=1), jnp.concatenate([zeros, kg], axis=1))
            c0 = g * Q_PER_KV * HEAD_DIM
            qg = q_ref[r0:r0 + BLOCK, c0:c0 + Q_PER_KV * HEAD_DIM] * scale
            qs = jnp.concatenate([qg[:, p * pair_w:(p + 1) * pair_w] for p in range(pairs)], axis=0)
            for par in range(2):
                s_scr[slot(qb, g, par)] = lax.dot_general(qs, kz[par], contract_lanes, preferred_element_type=F32)

    for qb in range(n_qb):
        pad_bias = jnp.where(first, NEG_INF, 0.0).astype(F32) if qb == 0 else None
        for g in range(N_KV_HEADS):
            for par in range(2):
                sl = slot(qb, g, par)
                for p in range(pairs):
                    rows = slice(p * BLOCK, (p + 1) * BLOCK)
                    s_prev = s_scr[sl, rows, :BLOCK]
                    if pad_bias is not None:
                        s_prev = s_prev + pad_bias
                    sm = jnp.where(upper, s_prev, s_scr[sl, rows, BLOCK:])
                    sink = sink_ref[g * Q_PER_KV + 2 * p + par]
                    m = jnp.maximum(jnp.max(sm, axis=-1, keepdims=True), sink)
                    m = jnp.broadcast_to(m, (BLOCK, BLOCK))
                    pe = jnp.exp(sm - m)
                    p_scr[sl, rows, :BLOCK] = jnp.where(upper, pe, 0.0).astype(BF16)
                    p_scr[sl, rows, BLOCK:] = jnp.where(upper, 0.0, pe).astype(BF16)
                    t_scr[sl, rows, :] = jnp.exp(sink - m)

    for qb in range(n_qb):
        r0 = qb * BLOCK
        win = window(qb)
        for g in range(N_KV_HEADS):
            vg = win[:, KV_WIDTH + g * HEAD_DIM:KV_WIDTH + (g + 1) * HEAD_DIM]
            vz = (jnp.concatenate([vg, zeros, ones], axis=1), jnp.concatenate([zeros, vg, ones], axis=1))
            c0 = g * Q_PER_KV * HEAD_DIM
            acc = None
            for par in range(2):
                sl = slot(qb, g, par)
                res = jnp.dot(p_scr[sl], vz[par], preferred_element_type=F32)
                on = res[:, :pair_w] / (res[:, pair_w:] + t_scr[sl])
                acc = on if acc is None else acc + on
            for p in range(pairs):
                o_ref[r0:r0 + BLOCK, c0 + p * pair_w:c0 + (p + 1) * pair_w] = (
                    acc[p * BLOCK:(p + 1) * BLOCK].astype(BF16))


def _attention(sinks, qkv, seq):
    t = qkv.shape[0]
    tq = 512
    tiles_per_seq = seq // tq
    kv_col = ATTN_WIDTH // (2 * KV_WIDTH)
    rb = tq // BLOCK
    slots = rb * N_KV_HEADS * 2
    rows = (Q_PER_KV // 2) * BLOCK
    return pl.pallas_call(
        functools.partial(_attn_kernel, tq=tq, tiles_per_seq=tiles_per_seq),
        grid=(t // tq,),
        in_specs=[pl.BlockSpec(memory_space=pltpu.SMEM),
                  pl.BlockSpec((tq, ATTN_WIDTH), lambda i: (i, 0)),
                  pl.BlockSpec((tq, 2 * KV_WIDTH), lambda i: (i, kv_col)),
                  pl.BlockSpec((BLOCK, 2 * KV_WIDTH), lambda i: (jnp.maximum(i * rb - 1, 0), kv_col))],
        out_specs=pl.BlockSpec((tq, ATTN_WIDTH), lambda i: (i, 0)),
        out_shape=jax.ShapeDtypeStruct((t, ATTN_WIDTH), BF16),
        scratch_shapes=[pltpu.VMEM((slots, rows, 2 * BLOCK), F32),
                        pltpu.VMEM((slots, rows, 2 * BLOCK), BF16),
                        pltpu.VMEM((slots, rows, BLOCK), F32)],
        compiler_params=_cparams(("arbitrary",), 40),
        name="attn",
    )(sinks, qkv, qkv, qkv)


def _ssmprep_kernel(ar_ref, ai_ref, ldt_ref, br_ref, bi_ref, cr_ref, ci_ref, lr_ref, li_ref, bw_ref, cw_ref):
    ar = ar_ref[...]
    ai = ai_ref[...]
    dt = jnp.exp(ldt_ref[...])
    mag = jnp.exp(ar * dt)
    lr = mag * jnp.cos(ai * dt)
    li = mag * jnp.sin(ai * dt)
    den = ar * ar + ai * ai
    zr = ((lr - 1.0) * ar + li * ai) / den
    zi = (li * ar - (lr - 1.0) * ai) / den
    lr_ref[...] = lr
    li_ref[...] = li

    def same_group(shape, row_w, col_w):
        rows = lax.broadcasted_iota(jnp.int32, shape, 0) // row_w
        cols = lax.broadcasted_iota(jnp.int32, shape, 1) // col_w
        return rows == cols

    b_mask = same_group((V7X_LANES, STATES_PER_BLOCK), SSM_GROUP, SSM_STATE)
    c_mask = same_group((STATES_PER_BLOCK, V7X_LANES), SSM_STATE, SSM_GROUP)
    for cb in range(CH_BLOCKS):
        gs = slice(cb * GROUPS_PER_BLOCK, (cb + 1) * GROUPS_PER_BLOCK)
        zr_b = zr[gs][:, None, :]
        zi_b = zi[gs][:, None, :]
        br = br_ref[gs]
        bi = bi_ref[gs]
        for part, bbar in enumerate((zr_b * br - zi_b * bi, zr_b * bi + zi_b * br)):
            rows = bbar.reshape(V7X_LANES, SSM_STATE)
            tiled = jnp.concatenate([rows] * GROUPS_PER_BLOCK, axis=1)
            bw_ref[cb, :, part * STATES_PER_BLOCK:(part + 1) * STATES_PER_BLOCK] = (
                jnp.where(b_mask, tiled, 0.0).astype(BF16))
        for part, c_ref in enumerate((cr_ref, ci_ref)):
            rows = c_ref[gs].reshape(STATES_PER_BLOCK, SSM_GROUP)
            tiled = jnp.concatenate([rows] * GROUPS_PER_BLOCK, axis=1)
            signed = tiled if part == 0 else -tiled
            cw_ref[cb, part * STATES_PER_BLOCK:(part + 1) * STATES_PER_BLOCK, :] = (
                jnp.where(c_mask, signed, 0.0).astype(BF16))


def _ssmprep(a_re, a_im, log_dt, b_re_t, b_im_t, c_re_t, c_im_t):
    g, n = a_re.shape
    return pl.pallas_call(
        _ssmprep_kernel,
        out_shape=[jax.ShapeDtypeStruct((g, n), F32), jax.ShapeDtypeStruct((g, n), F32),
                   jax.ShapeDtypeStruct((CH_BLOCKS, V7X_LANES, 2 * STATES_PER_BLOCK), BF16),
                   jax.ShapeDtypeStruct((CH_BLOCKS, 2 * STATES_PER_BLOCK, V7X_LANES), BF16)],
        name="ssmprep",
    )(a_re, a_im, log_dt, b_re_t, b_im_t, c_re_t, c_im_t)


SLAB_PITCH = 2 * STATE_ROWS + 4


def _ssm_kernel(u_ref, bw_ref, cw_ref, lr_ref, li_ref, d_ref, gy_ref, hcr_ref, hci_ref, *bufs, tc, nb):
    tiles = tc // V7X_SUBLANES

    @pl.when(pl.program_id(0) == 0)
    def _():
        hcr_ref[...] = jnp.zeros_like(hcr_ref)
        hci_ref[...] = jnp.zeros_like(hci_ref)

    def tile_rows(ti, state_tile, imag):
        return pl.ds(ti * V7X_SUBLANES * SLAB_PITCH + 2 * state_tile + imag, V7X_SUBLANES, stride=SLAB_PITCH)

    def slab_rows(t, imag):
        return pl.ds(t * SLAB_PITCH + imag, STATE_ROWS, stride=2)

    def project_in(b):
        x_ref = bufs[2 * b]
        ub = u_ref[b].astype(BF16)
        for cb in range(CH_BLOCKS):
            res = jnp.dot(ub[:, cb * V7X_LANES:(cb + 1) * V7X_LANES], bw_ref[cb], preferred_element_type=F32)
            for k in range(TILES_PER_BLOCK):
                for ti in range(tiles):
                    rs = slice(ti * V7X_SUBLANES, (ti + 1) * V7X_SUBLANES)
                    st = cb * TILES_PER_BLOCK + k
                    x_ref[tile_rows(ti, st, 0), :] = res[rs, k * V7X_LANES:(k + 1) * V7X_LANES]
                    x_ref[tile_rows(ti, st, 1), :] = res[rs, STATES_PER_BLOCK + k * V7X_LANES:
                                                         STATES_PER_BLOCK + (k + 1) * V7X_LANES]

    def recur(b):
        x_ref, h_ref = bufs[2 * b], bufs[2 * b + 1]
        lr = lr_ref[...]
        li = li_ref[...]
        hr = hcr_ref[b]
        hi = hci_ref[b]
        for t in range(tc):
            nr = lr * hr - li * hi + x_ref[slab_rows(t, 0), :]
            ni = lr * hi + li * hr + x_ref[slab_rows(t, 1), :]
            h_ref[slab_rows(t, 0), :] = nr
            h_ref[slab_rows(t, 1), :] = ni
            hr, hi = nr, ni
        hcr_ref[b] = hr
        hci_ref[b] = hi

    def project_out(b):
        h_ref = bufs[2 * b + 1]
        u = u_ref[b]
        for cb in range(CH_BLOCKS):
            row_tiles = []
            for ti in range(tiles):
                parts = [h_ref[tile_rows(ti, cb * TILES_PER_BLOCK + k, imag), :]
                         for imag in (0, 1) for k in range(TILES_PER_BLOCK)]
                row_tiles.append(jnp.concatenate(parts, axis=-1))
            hcat = jnp.concatenate(row_tiles, axis=0).astype(BF16)
            y = jnp.dot(hcat, cw_ref[cb], preferred_element_type=F32)
            cols = slice(cb * V7X_LANES, (cb + 1) * V7X_LANES)
            y = y + d_ref[:, cols] * u[:, cols]
            gy_ref[b, :, cols] = _gelu_tanh(y).astype(BF16)

    for b in range(nb):
        project_in(b)
    for b in range(nb):
        recur(b)
        project_out(b)


def _ssm(u3, bw, cw, lam_r, lam_i, dskip):
    nb, seq, _ = u3.shape
    tc = 128
    slab_buf = pltpu.VMEM((tc * SLAB_PITCH, V7X_LANES), F32)
    return pl.pallas_call(
        functools.partial(_ssm_kernel, tc=tc, nb=nb),
        grid=(seq // tc,),
        in_specs=[pl.BlockSpec((nb, tc, SSM_WIDTH), lambda j: (0, j, 0)),
                  pl.BlockSpec(bw.shape, lambda j: (0, 0, 0)),
                  pl.BlockSpec(cw.shape, lambda j: (0, 0, 0)),
                  pl.BlockSpec((STATE_ROWS, V7X_LANES), lambda j: (0, 0)),
                  pl.BlockSpec((STATE_ROWS, V7X_LANES), lambda j: (0, 0)),
                  pl.BlockSpec((1, SSM_WIDTH), lambda j: (0, 0))],
        out_specs=pl.BlockSpec((nb, tc, SSM_WIDTH), lambda j: (0, j, 0)),
        out_shape=jax.ShapeDtypeStruct((nb, seq, SSM_WIDTH), BF16),
        scratch_shapes=[pltpu.VMEM((nb, STATE_ROWS, V7X_LANES), F32),
                        pltpu.VMEM((nb, STATE_ROWS, V7X_LANES), F32)] + [slab_buf] * (2 * nb),
        compiler_params=_cparams(("arbitrary",), 48),
        name="ssm",
    )(u3, bw, cw, lam_r, lam_i, dskip)


def _mix_kernel(x_ref, h_ref, mod_ref, g2n_ref, o_ref, gy_ref, wga_ref, wgs_ref, wap_ref, wla_ref, wlb_ref,
                wout_ref, x1_ref, h2_ref):
    n = pl.program_id(1)

    @pl.when(n == 0)
    def _():
        x1_ref[...] = jnp.zeros_like(x1_ref)

    h = h_ref[...]
    ga = jnp.dot(h, wga_ref[...], preferred_element_type=F32)
    gs = jnp.dot(h, wgs_ref[...], preferred_element_type=F32)
    at = jnp.dot(o_ref[...], wap_ref[...], preferred_element_type=F32)
    gy = gy_ref[...]
    la = jnp.dot(gy, wla_ref[...], preferred_element_type=F32)
    lb = jnp.dot(gy, wlb_ref[...], preferred_element_type=F32)
    ssm = la * _sigmoid(lb)
    mixed = _sigmoid(ga) * at + _sigmoid(gs) * ssm
    x1_ref[...] += jnp.dot(mixed.astype(BF16), wout_ref[...], preferred_element_type=F32)

    @pl.when(n == pl.num_programs(1) - 1)
    def _():
        x1 = x_ref[...] + mod_ref[2:3, :] * x1_ref[...]
        x1_ref[...] = x1
        h2_ref[...] = _modulated_norm(x1, g2n_ref[...], mod_ref[4:5, :], mod_ref[3:4, :]).astype(BF16)


def _mix(x2, h, mod, g2n, o, gy3, wgates, wap, wglu, wout, seq):
    t = x2.shape[0]
    tm, tn = 512, 512
    tpb = seq // tm
    nn = D_MODEL // tn
    return pl.pallas_call(
        _mix_kernel,
        grid=(t // tm, nn),
        in_specs=[pl.BlockSpec((tm, D_MODEL), lambda i, n: (i, 0)),
                  pl.BlockSpec((tm, D_MODEL), lambda i, n: (i, 0)),
                  pl.BlockSpec((None, 6, D_MODEL), lambda i, n: (i // tpb, 0, 0)),
                  pl.BlockSpec((1, D_MODEL), lambda i, n: (0, 0)),
                  pl.BlockSpec((tm, ATTN_WIDTH), lambda i, n: (i, 0)),
                  pl.BlockSpec((None, tm, SSM_WIDTH), lambda i, n: (i // tpb, i % tpb, 0)),
                  pl.BlockSpec((D_MODEL, tn), lambda i, n: (0, n)),
                  pl.BlockSpec((D_MODEL, tn), lambda i, n: (0, nn + n)),
                  pl.BlockSpec((ATTN_WIDTH, tn), lambda i, n: (0, n)),
                  pl.BlockSpec((SSM_WIDTH, tn), lambda i, n: (0, n)),
                  pl.BlockSpec((SSM_WIDTH, tn), lambda i, n: (0, nn + n)),
                  pl.BlockSpec((tn, D_MODEL), lambda i, n: (n, 0))],
        out_specs=[pl.BlockSpec((tm, D_MODEL), lambda i, n: (i, 0)),
                   pl.BlockSpec((tm, D_MODEL), lambda i, n: (i, 0))],
        out_shape=[jax.ShapeDtypeStruct((t, D_MODEL), F32),
                   jax.ShapeDtypeStruct((t, D_MODEL), BF16)],
        compiler_params=_cparams(("arbitrary", "arbitrary"), 56),
        name="mix",
    )(x2, h, mod, g2n, o, gy3, wgates, wgates, wap, wglu, wglu, wout)


def _ffn_kernel(h_ref, halo_ref, x1_ref, mod_ref, wg_ref, wv_ref, cw_ref, cb_ref, wd_ref, gf_ref, out_ref,
                gate_scr, act_even, act_odd, *, tm, n_tiles, tiles_per_seq, final_norm):
    f = pl.program_id(1)
    first = (pl.program_id(0) % tiles_per_seq) == 0
    act_bufs = (act_even, act_odd)

    def down_stage(act_ref):
        out_ref[...] += jnp.dot(act_ref[...], wd_ref[...].astype(BF16), preferred_element_type=F32)

    def step(act_out, act_in):
        h = h_ref[...]
        wg = wg_ref[...].astype(BF16)
        gate = jnp.dot(h, wg, preferred_element_type=F32)
        val = jnp.dot(h, wv_ref[...].astype(BF16), preferred_element_type=F32)
        halo = jnp.dot(halo_ref[...], wg, preferred_element_type=F32)
        if act_in is not None:
            down_stage(act_in)
        gate_scr[0:V7X_SUBLANES, :] = jnp.where(first, 0.0, halo)
        gate_scr[V7X_SUBLANES:V7X_SUBLANES + tm, :] = gate
        g1 = gate_scr[V7X_SUBLANES - 1:V7X_SUBLANES - 1 + tm, :]
        g2 = gate_scr[V7X_SUBLANES - 2:V7X_SUBLANES - 2 + tm, :]
        conv = cw_ref[0:1, :] * g2 + cw_ref[1:2, :] * g1 + cw_ref[2:3, :] * gate + cb_ref[...]
        act_out[...] = (_silu(conv) * val).astype(BF16)

    @pl.when(f == 0)
    def _():
        out_ref[...] = jnp.zeros_like(out_ref)
        step(act_bufs[0], None)

    for parity in range(2):
        @pl.when((f > 0) & (f < n_tiles) & (f % 2 == parity))
        def _():
            step(act_bufs[parity], act_bufs[1 - parity])

    @pl.when(f == n_tiles)
    def _():
        down_stage(act_bufs[(n_tiles - 1) % 2])
        x2 = x1_ref[...] + mod_ref[5:6, :] * out_ref[...]
        out_ref[...] = _rms(x2) * gf_ref[...] if final_norm else x2


def _ffn(h2, x1, mod, wup, conv_w, conv_b, wdown, gf, seq, final_norm):
    t = h2.shape[0]
    tm, tf = 1024, 256
    tpb = seq // tm
    nf = D_FF // tf
    hb = tm // V7X_SUBLANES
    return pl.pallas_call(
        functools.partial(_ffn_kernel, tm=tm, n_tiles=nf, tiles_per_seq=tpb, final_norm=final_norm),
        grid=(t // tm, nf + 1),
        in_specs=[pl.BlockSpec((tm, D_MODEL), lambda i, f: (i, 0)),
                  pl.BlockSpec((V7X_SUBLANES, D_MODEL), lambda i, f: (jnp.maximum(i * hb - 1, 0), 0)),
                  pl.BlockSpec((tm, D_MODEL), lambda i, f: (i, 0), pipeline_mode=pl.Buffered(1)),
                  pl.BlockSpec((None, 6, D_MODEL), lambda i, f: (i // tpb, 0, 0)),
                  pl.BlockSpec((D_MODEL, tf), lambda i, f: (0, jnp.minimum(f, nf - 1))),
                  pl.BlockSpec((D_MODEL, tf), lambda i, f: (0, nf + jnp.minimum(f, nf - 1))),
                  pl.BlockSpec((3, tf), lambda i, f: (0, jnp.minimum(f, nf - 1))),
                  pl.BlockSpec((1, tf), lambda i, f: (0, jnp.minimum(f, nf - 1))),
                  pl.BlockSpec((tf, D_MODEL), lambda i, f: (jnp.maximum(f - 1, 0), 0)),
                  pl.BlockSpec((1, D_MODEL), lambda i, f: (0, 0))],
        out_specs=pl.BlockSpec((tm, D_MODEL), lambda i, f: (i, 0)),
        out_shape=jax.ShapeDtypeStruct((t, D_MODEL), F32),
        scratch_shapes=[pltpu.VMEM((tm + V7X_SUBLANES, tf), F32), pltpu.VMEM((tm, tf), BF16), pltpu.VMEM((tm, tf), BF16)],
        compiler_params=_cparams(("arbitrary", "arbitrary"), 60),
        name="ffn",
    )(h2, h2, x1, mod, wup, wup, conv_w, conv_b, wdown, gf)


def kernel(x, c, ada_w, ada_b, norm_mix_g, w_in, attn_sinks, w_attn_proj, ssm_a_re, ssm_a_im, ssm_log_dt,
           ssm_b_re, ssm_b_im, ssm_c_re, ssm_c_im, ssm_d, w_ssm_glu, w_out, norm_ffn_g, w_ffn_up, ffn_conv_w,
           ffn_conv_b, w_ffn_down, final_g):
    bsz, seq, _ = x.shape
    x2 = x.reshape(bsz * seq, D_MODEL)
    c_pad = jnp.pad(c, ((0, V7X_SUBLANES - bsz), (0, 0)))
    for i in range(ada_w.shape[0]):
        mod = _ada(c_pad, ada_w[i], ada_b[i].reshape(1, -1))[:bsz].reshape(bsz, 6, D_MODEL)

        win = w_in[i]
        w_gates = win[:, QKV_WIDTH + SSM_WIDTH:].astype(BF16)
        g1n = norm_mix_g[i].reshape(1, D_MODEL)
        g2n = norm_ffn_g[i].reshape(1, D_MODEL)

        h, qkv, u = _inproj(x2, mod, g1n, win, seq)
        o = _attention(attn_sinks[i], qkv, seq)

        lam_r, lam_i, bw, cw = _ssmprep(
            ssm_a_re[i], ssm_a_im[i], ssm_log_dt[i].reshape(-1, 1),
            ssm_b_re[i].transpose(0, 2, 1), ssm_b_im[i].transpose(0, 2, 1),
            ssm_c_re[i].transpose(0, 2, 1), ssm_c_im[i].transpose(0, 2, 1))
        gy = _ssm(u.reshape(bsz, seq, SSM_WIDTH), bw, cw,
                  lam_r.reshape(STATE_ROWS, V7X_LANES), lam_i.reshape(STATE_ROWS, V7X_LANES),
                  ssm_d[i].reshape(1, SSM_WIDTH))

        x2, h2 = _mix(x2, h, mod, g2n, o, gy, w_gates,
                      w_attn_proj[i].astype(BF16), w_ssm_glu[i].astype(BF16), w_out[i].astype(BF16), seq)
        x2 = _ffn(h2, x2, mod, w_ffn_up[i], ffn_conv_w[i], ffn_conv_b[i].reshape(1, D_FF),
                  w_ffn_down[i], final_g.reshape(1, D_MODEL), seq, i == ada_w.shape[0] - 1)
    return x2.reshape(bsz, seq, D_MODEL)
```

```python
import functools

import jax
import jax.numpy as jnp
import numpy as np
from jax import lax
from jax.experimental import pallas as pl
from jax.experimental.pallas import tpu as pltpu

F32 = jnp.float32
BF16 = jnp.bfloat16

D_MODEL = 2048
HEAD_DIM = 64
N_Q_HEADS = 16
N_KV_HEADS = 2
Q_PER_KV = N_Q_HEADS // N_KV_HEADS
ATTN_WIDTH = N_Q_HEADS * HEAD_DIM
KV_WIDTH = N_KV_HEADS * HEAD_DIM
QKV_WIDTH = ATTN_WIDTH + 2 * KV_WIDTH
WINDOW = 128
BLOCK = 128
NEG_INF = -1e30
SSM_WIDTH = D_MODEL // 2
SSM_GROUP = 16
N_SSM_GROUPS = SSM_WIDTH // SSM_GROUP
SSM_STATE = 64
N_STATES = N_SSM_GROUPS * SSM_STATE
D_FF = 5632
RMS_EPS = 1e-6

V7X_SUBLANES = 8
V7X_LANES = 128
STATE_ROWS = N_STATES // V7X_LANES
CH_BLOCKS = SSM_WIDTH // V7X_LANES
GROUPS_PER_BLOCK = V7X_LANES // SSM_GROUP
STATES_PER_BLOCK = GROUPS_PER_BLOCK * SSM_STATE
TILES_PER_BLOCK = STATES_PER_BLOCK // V7X_LANES


def _cparams(sem, vmem_mb):
    return pltpu.CompilerParams(dimension_semantics=sem, vmem_limit_bytes=vmem_mb * 1024 * 1024)


def _sigmoid(x):
    return 1.0 / (1.0 + jnp.exp(-x))


def _silu(x):
    return x * _sigmoid(x)


def _gelu_tanh(x):
    c = np.float32(np.sqrt(2.0 / np.pi))
    return x * (0.5 * (1.0 + jnp.tanh(c * (x + np.float32(0.044715) * (x * x * x)))))


def _rms(x):
    return x * lax.rsqrt(jnp.mean(x * x, axis=-1, keepdims=True) + RMS_EPS)


def _modulated_norm(x, g, scale, shift):
    return _rms(x) * g * (1.0 + scale) + shift


N_MOD = 6
MOD_ROWS = V7X_SUBLANES


def _ada_kernel(c_ref, w_ref, b_ref, o_ref):
    c = c_ref[...]
    cond = jnp.concatenate([_silu(c), jnp.zeros((MOD_ROWS - c.shape[0], c.shape[1]), F32)], axis=0)
    o_ref[...] = jnp.dot(cond.astype(BF16), w_ref[...].astype(BF16), preferred_element_type=F32) + b_ref[...]


def _ada(c, w, b):
    assert c.shape[0] <= MOD_ROWS and w.shape[1] == N_MOD * D_MODEL
    tn = 1024
    per = D_MODEL // tn
    return pl.pallas_call(
        _ada_kernel,
        grid=(N_MOD * per,),
        in_specs=[pl.BlockSpec(c.shape, lambda j: (0, 0)),
                  pl.BlockSpec((D_MODEL, tn), lambda j: (0, j)),
                  pl.BlockSpec((1, tn), lambda j: (0, j))],
        out_specs=pl.BlockSpec((None, MOD_ROWS, tn), lambda j: (j // per, 0, j % per)),
        out_shape=jax.ShapeDtypeStruct((N_MOD, MOD_ROWS, D_MODEL), F32),
        compiler_params=_cparams(("arbitrary",), 40),
        name="ada",
    )(c, w, b)


def _mod_spec():
    return pl.BlockSpec((N_MOD, MOD_ROWS, D_MODEL), lambda *_: (0, 0, 0))


def _mod_row(mod_ref, k, b):
    return mod_ref[k, pl.ds(b, 1), :]


def _inproj_kernel(x_ref, mod_ref, g_ref, w_ref, wgate_ref, h_ref, qkv_ref, u_ref, wgate_bf_ref, wb_scr,
                   *, tiles_per_seq):
    @pl.when(pl.program_id(0) == 0)
    def _():
        wb_scr[...] = w_ref[...].astype(BF16)

    wgate_bf_ref[...] = wgate_ref[...].astype(BF16)

    b = pl.program_id(0) // tiles_per_seq
    h = _modulated_norm(x_ref[...], g_ref[...], _mod_row(mod_ref, 1, b), _mod_row(mod_ref, 0, b)).astype(BF16)
    h_ref[...] = h
    p = jnp.dot(h, wb_scr[...], preferred_element_type=F32)
    qkv_ref[...] = p[:, :QKV_WIDTH].astype(BF16)
    u_ref[...] = p[:, QKV_WIDTH:]


def _inproj(x2, mod, g, w_in, seq):
    t = x2.shape[0]
    tm = 512
    tpb = seq // tm
    n = QKV_WIDTH + SSM_WIDTH
    steps = t // tm
    gate_cols = w_in.shape[1] - n
    gw = gate_cols // steps
    assert gate_cols % steps == 0 and n % gw == 0 and gw % V7X_LANES == 0
    return pl.pallas_call(
        functools.partial(_inproj_kernel, tiles_per_seq=tpb),
        grid=(steps,),
        in_specs=[pl.BlockSpec((tm, D_MODEL), lambda i: (i, 0)),
                  _mod_spec(),
                  pl.BlockSpec((1, D_MODEL), lambda i: (0, 0)),
                  pl.BlockSpec((D_MODEL, n), lambda i: (0, 0), pipeline_mode=pl.Buffered(1)),
                  pl.BlockSpec((D_MODEL, gw), lambda i: (0, n // gw + i))],
        out_specs=[pl.BlockSpec((tm, D_MODEL), lambda i: (i, 0)),
                   pl.BlockSpec((tm, QKV_WIDTH), lambda i: (i, 0)),
                   pl.BlockSpec((tm, SSM_WIDTH), lambda i: (i, 0)),
                   pl.BlockSpec((D_MODEL, gw), lambda i: (0, i))],
        out_shape=[jax.ShapeDtypeStruct((t, D_MODEL), BF16),
                   jax.ShapeDtypeStruct((t, QKV_WIDTH), BF16),
                   jax.ShapeDtypeStruct((t, SSM_WIDTH), F32),
                   jax.ShapeDtypeStruct((D_MODEL, gate_cols), BF16)],
        scratch_shapes=[pltpu.VMEM((D_MODEL, n), BF16)],
        compiler_params=_cparams(("arbitrary",), 60),
        name="inproj",
    )(x2, mod, g, w_in, w_in)


def _attn_kernel(sink_ref, q_ref, kvc_ref, kvp_ref, w0_ref, w1_ref, w2_ref, o_ref, w0b_ref, w1b_ref, w2b_ref,
                 s_scr, p_scr, t_scr, *, tq, tiles_per_seq):
    for src, dst in ((w0_ref, w0b_ref), (w1_ref, w1b_ref), (w2_ref, w2b_ref)):
        dst[...] = src[...].astype(BF16)

    first = (pl.program_id(0) % tiles_per_seq) == 0
    qi = lax.broadcasted_iota(jnp.int32, (BLOCK, BLOCK), 0)
    kj = lax.broadcasted_iota(jnp.int32, (BLOCK, BLOCK), 1)
    upper = kj > qi
    pairs = Q_PER_KV // 2
    pair_w = 2 * HEAD_DIM
    zeros = jnp.zeros((2 * BLOCK, HEAD_DIM), BF16)
    ones = jnp.ones((2 * BLOCK, pair_w), BF16)
    scale = jnp.asarray(HEAD_DIM ** -0.5, BF16)
    contract_lanes = (((1,), (1,)), ((), ()))
    n_qb = tq // BLOCK

    def window(qb):
        r0 = qb * BLOCK
        prev = kvp_ref[...] if qb == 0 else kvc_ref[r0 - BLOCK:r0, :]
        return jnp.concatenate([prev, kvc_ref[r0:r0 + BLOCK, :]], axis=0)

    def slot(qb, g, par):
        return (qb * N_KV_HEADS + g) * 2 + par

    for qb in range(n_qb):
        r0 = qb * BLOCK
        win = window(qb)
        for g in range(N_KV_HEADS):
            kg = win[:, g * HEAD_DIM:(g + 1) * HEAD_DIM]
            kz = (jnp.concatenate([kg, zeros], axis=1), jnp.concatenate([zeros, kg], axis=1))
            c0 = g * Q_PER_KV * HEAD_DIM
            qg = q_ref[r0:r0 + BLOCK, c0:c0 + Q_PER_KV * HEAD_DIM] * scale
            qs = jnp.concatenate([qg[:, p * pair_w:(p + 1) * pair_w] for p in range(pairs)], axis=0)
            for par in range(2):
                s_scr[slot(qb, g, par)] = lax.dot_general(qs, kz[par], contract_lanes, preferred_element_type=F32)

    for qb in range(n_qb):
        pad_bias = jnp.where(first, NEG_INF, 0.0).astype(F32) if qb == 0 else None
        for g in range(N_KV_HEADS):
            for par in range(2):
                sl = slot(qb, g, par)
                for p in range(pairs):
                    rows = slice(p * BLOCK, (p + 1) * BLOCK)
                    s_prev = s_scr[sl, rows, :BLOCK]
                    if pad_bias is not None:
                        s_prev = s_prev + pad_bias
                    sm = jnp.where(upper, s_prev, s_scr[sl, rows, BLOCK:])
                    sink = sink_ref[g * Q_PER_KV + 2 * p + par]
                    m = jnp.maximum(jnp.max(sm, axis=-1, keepdims=True), sink)
                    m = jnp.broadcast_to(m, (BLOCK, BLOCK))
                    pe = jnp.exp(sm - m)
                    p_scr[sl, rows, :BLOCK] = jnp.where(upper, pe, 0.0).astype(BF16)
                    p_scr[sl, rows, BLOCK:] = jnp.where(upper, 0.0, pe).astype(BF16)
                    t_scr[sl, rows, :] = jnp.exp(sink - m)

    for qb in range(n_qb):
        r0 = qb * BLOCK
        win = window(qb)
        for g in range(N_KV_HEADS):
            vg = win[:, KV_WIDTH + g * HEAD_DIM:KV_WIDTH + (g + 1) * HEAD_DIM]
            vz = (jnp.concatenate([vg, zeros, ones], axis=1), jnp.concatenate([zeros, vg, ones], axis=1))
            c0 = g * Q_PER_KV * HEAD_DIM
            acc = None
            for par in range(2):
                sl = slot(qb, g, par)
                res = jnp.dot(p_scr[sl], vz[par], preferred_element_type=F32)
                on = res[:, :pair_w] / (res[:, pair_w:] + t_scr[sl])
                acc = on if acc is None else acc + on
            for p in range(pairs):
                o_ref[r0:r0 + BLOCK, c0 + p * pair_w:c0 + (p + 1) * pair_w] = (
                    acc[p * BLOCK:(p + 1) * BLOCK].astype(BF16))


def _attention(sinks, qkv, seq, weights):
    t = qkv.shape[0]
    tq = 512
    tiles_per_seq = seq // tq
    kv_col = ATTN_WIDTH // (2 * KV_WIDTH)
    rb = tq // BLOCK
    steps = t // tq
    slots = rb * N_KV_HEADS * 2
    rows = (Q_PER_KV // 2) * BLOCK
    for w in weights:
        assert w.shape[1] % (steps * V7X_LANES) == 0
    w_specs = [pl.BlockSpec((w.shape[0], w.shape[1] // steps), lambda i: (0, i)) for w in weights]
    return pl.pallas_call(
        functools.partial(_attn_kernel, tq=tq, tiles_per_seq=tiles_per_seq),
        grid=(steps,),
        in_specs=[pl.BlockSpec(memory_space=pltpu.SMEM),
                  pl.BlockSpec((tq, ATTN_WIDTH), lambda i: (i, 0)),
                  pl.BlockSpec((tq, 2 * KV_WIDTH), lambda i: (i, kv_col)),
                  pl.BlockSpec((BLOCK, 2 * KV_WIDTH), lambda i: (jnp.maximum(i * rb - 1, 0), kv_col))] + w_specs,
        out_specs=[pl.BlockSpec((tq, ATTN_WIDTH), lambda i: (i, 0))] + w_specs,
        out_shape=[jax.ShapeDtypeStruct((t, ATTN_WIDTH), BF16)]
                  + [jax.ShapeDtypeStruct(w.shape, BF16) for w in weights],
        scratch_shapes=[pltpu.VMEM((slots, rows, 2 * BLOCK), F32),
                        pltpu.VMEM((slots, rows, 2 * BLOCK), BF16),
                        pltpu.VMEM((slots, rows, BLOCK), F32)],
        compiler_params=_cparams(("arbitrary",), 48),
        name="attn",
    )(sinks, qkv, qkv, qkv, *weights)


def _ssmprep_kernel(ar_ref, ai_ref, ldt_ref, br_ref, bi_ref, cr_ref, ci_ref, lr_ref, li_ref, bw_ref, cw_ref):
    ar = ar_ref[...]
    ai = ai_ref[...]
    dt = jnp.exp(ldt_ref[...])
    mag = jnp.exp(ar * dt)
    lr = mag * jnp.cos(ai * dt)
    li = mag * jnp.sin(ai * dt)
    den = ar * ar + ai * ai
    zr = ((lr - 1.0) * ar + li * ai) / den
    zi = (li * ar - (lr - 1.0) * ai) / den
    lr_ref[...] = lr
    li_ref[...] = li

    def same_group(shape, row_w, col_w):
        rows = lax.broadcasted_iota(jnp.int32, shape, 0) // row_w
        cols = lax.broadcasted_iota(jnp.int32, shape, 1) // col_w
        return rows == cols

    b_mask = same_group((V7X_LANES, STATES_PER_BLOCK), SSM_GROUP, SSM_STATE)
    c_mask = same_group((STATES_PER_BLOCK, V7X_LANES), SSM_STATE, SSM_GROUP)
    for cb in range(CH_BLOCKS):
        gs = slice(cb * GROUPS_PER_BLOCK, (cb + 1) * GROUPS_PER_BLOCK)
        zr_b = zr[gs][:, None, :]
        zi_b = zi[gs][:, None, :]
        br = br_ref[gs]
        bi = bi_ref[gs]
        for part, bbar in enumerate((zr_b * br - zi_b * bi, zr_b * bi + zi_b * br)):
            rows = bbar.reshape(V7X_LANES, SSM_STATE)
            tiled = jnp.concatenate([rows] * GROUPS_PER_BLOCK, axis=1)
            bw_ref[cb, :, part * STATES_PER_BLOCK:(part + 1) * STATES_PER_BLOCK] = (
                jnp.where(b_mask, tiled, 0.0).astype(BF16))
        for part, c_ref in enumerate((cr_ref, ci_ref)):
            rows = c_ref[gs].reshape(STATES_PER_BLOCK, SSM_GROUP)
            tiled = jnp.concatenate([rows] * GROUPS_PER_BLOCK, axis=1)
            signed = tiled if part == 0 else -tiled
            cw_ref[cb, part * STATES_PER_BLOCK:(part + 1) * STATES_PER_BLOCK, :] = (
                jnp.where(c_mask, signed, 0.0).astype(BF16))


def _ssmprep(a_re, a_im, log_dt, b_re_t, b_im_t, c_re_t, c_im_t):
    g, n = a_re.shape
    return pl.pallas_call(
        _ssmprep_kernel,
        out_shape=[jax.ShapeDtypeStruct((g, n), F32), jax.ShapeDtypeStruct((g, n), F32),
                   jax.ShapeDtypeStruct((CH_BLOCKS, V7X_LANES, 2 * STATES_PER_BLOCK), BF16),
                   jax.ShapeDtypeStruct((CH_BLOCKS, 2 * STATES_PER_BLOCK, V7X_LANES), BF16)],
        name="ssmprep",
    )(a_re, a_im, log_dt, b_re_t, b_im_t, c_re_t, c_im_t)


SLAB_PITCH = 2 * STATE_ROWS + 4


def _ssm_kernel(u_ref, bw_ref, cw_ref, lr_ref, li_ref, d_ref, gy_ref, hcr_ref, hci_ref, *bufs, tc, nb):
    tiles = tc // V7X_SUBLANES

    @pl.when(pl.program_id(0) == 0)
    def _():
        hcr_ref[...] = jnp.zeros_like(hcr_ref)
        hci_ref[...] = jnp.zeros_like(hci_ref)

    def tile_rows(ti, state_tile, imag):
        return pl.ds(ti * V7X_SUBLANES * SLAB_PITCH + 2 * state_tile + imag, V7X_SUBLANES, stride=SLAB_PITCH)

    def slab_rows(t, imag):
        return pl.ds(t * SLAB_PITCH + imag, STATE_ROWS, stride=2)

    def project_in(b):
        x_ref = bufs[2 * b]
        ub = u_ref[b].astype(BF16)
        for cb in range(CH_BLOCKS):
            res = jnp.dot(ub[:, cb * V7X_LANES:(cb + 1) * V7X_LANES], bw_ref[cb], preferred_element_type=F32)
            for k in range(TILES_PER_BLOCK):
                for ti in range(tiles):
                    rs = slice(ti * V7X_SUBLANES, (ti + 1) * V7X_SUBLANES)
                    st = cb * TILES_PER_BLOCK + k
                    x_ref[tile_rows(ti, st, 0), :] = res[rs, k * V7X_LANES:(k + 1) * V7X_LANES]
                    x_ref[tile_rows(ti, st, 1), :] = res[rs, STATES_PER_BLOCK + k * V7X_LANES:
                                                         STATES_PER_BLOCK + (k + 1) * V7X_LANES]

    def recur(b):
        x_ref, h_ref = bufs[2 * b], bufs[2 * b + 1]
        lr = lr_ref[...]
        li = li_ref[...]
        hr = hcr_ref[b]
        hi = hci_ref[b]
        for t in range(tc):
            nr = lr * hr - li * hi + x_ref[slab_rows(t, 0), :]
            ni = lr * hi + li * hr + x_ref[slab_rows(t, 1), :]
            h_ref[slab_rows(t, 0), :] = nr
            h_ref[slab_rows(t, 1), :] = ni
            hr, hi = nr, ni
        hcr_ref[b] = hr
        hci_ref[b] = hi

    def project_out(b):
        h_ref = bufs[2 * b + 1]
        u = u_ref[b]
        for cb in range(CH_BLOCKS):
            row_tiles = []
            for ti in range(tiles):
                parts = [h_ref[tile_rows(ti, cb * TILES_PER_BLOCK + k, imag), :]
                         for imag in (0, 1) for k in range(TILES_PER_BLOCK)]
                row_tiles.append(jnp.concatenate(parts, axis=-1))
            hcat = jnp.concatenate(row_tiles, axis=0).astype(BF16)
            y = jnp.dot(hcat, cw_ref[cb], preferred_element_type=F32)
            cols = slice(cb * V7X_LANES, (cb + 1) * V7X_LANES)
            y = y + d_ref[:, cols] * u[:, cols]
            gy_ref[b, :, cols] = _gelu_tanh(y).astype(BF16)

    for b in range(nb):
        project_in(b)
    for b in range(nb):
        recur(b)
        project_out(b)


def _ssm(u3, bw, cw, lam_r, lam_i, dskip):
    nb, seq, _ = u3.shape
    tc = 128
    slab_buf = pltpu.VMEM((tc * SLAB_PITCH, V7X_LANES), F32)
    return pl.pallas_call(
        functools.partial(_ssm_kernel, tc=tc, nb=nb),
        grid=(seq // tc,),
        in_specs=[pl.BlockSpec((nb, tc, SSM_WIDTH), lambda j: (0, j, 0)),
                  pl.BlockSpec(bw.shape, lambda j: (0, 0, 0)),
                  pl.BlockSpec(cw.shape, lambda j: (0, 0, 0)),
                  pl.BlockSpec((STATE_ROWS, V7X_LANES), lambda j: (0, 0)),
                  pl.BlockSpec((STATE_ROWS, V7X_LANES), lambda j: (0, 0)),
                  pl.BlockSpec((1, SSM_WIDTH), lambda j: (0, 0))],
        out_specs=pl.BlockSpec((nb, tc, SSM_WIDTH), lambda j: (0, j, 0)),
        out_shape=jax.ShapeDtypeStruct((nb, seq, SSM_WIDTH), BF16),
        scratch_shapes=[pltpu.VMEM((nb, STATE_ROWS, V7X_LANES), F32),
                        pltpu.VMEM((nb, STATE_ROWS, V7X_LANES), F32)] + [slab_buf] * (2 * nb),
        compiler_params=_cparams(("arbitrary",), 48),
        name="ssm",
    )(u3, bw, cw, lam_r, lam_i, dskip)


def _mix_kernel(x_ref, h_ref, mod_ref, g2n_ref, o_ref, gy_ref, wga_ref, wgs_ref, wap_ref, wla_ref, wlb_ref,
                wout_ref, x1_ref, h2_ref, *, tiles_per_seq):
    n = pl.program_id(1)
    b = pl.program_id(0) // tiles_per_seq

    @pl.when(n == 0)
    def _():
        x1_ref[...] = jnp.zeros_like(x1_ref)

    h = h_ref[...]
    ga = jnp.dot(h, wga_ref[...], preferred_element_type=F32)
    gs = jnp.dot(h, wgs_ref[...], preferred_element_type=F32)
    at = jnp.dot(o_ref[...], wap_ref[...], preferred_element_type=F32)
    gy = gy_ref[...]
    la = jnp.dot(gy, wla_ref[...], preferred_element_type=F32)
    lb = jnp.dot(gy, wlb_ref[...], preferred_element_type=F32)
    ssm = la * _sigmoid(lb)
    mixed = _sigmoid(ga) * at + _sigmoid(gs) * ssm
    x1_ref[...] += jnp.dot(mixed.astype(BF16), wout_ref[...], preferred_element_type=F32)

    @pl.when(n == pl.num_programs(1) - 1)
    def _():
        x1 = x_ref[...] + _mod_row(mod_ref, 2, b) * x1_ref[...]
        x1_ref[...] = x1
        h2_ref[...] = _modulated_norm(x1, g2n_ref[...], _mod_row(mod_ref, 4, b), _mod_row(mod_ref, 3, b)).astype(BF16)


def _mix(x2, h, mod, g2n, o, gy3, wgates, wap, wglu, wout, seq):
    t = x2.shape[0]
    tm, tn = 512, 512
    tpb = seq // tm
    nn = D_MODEL // tn
    return pl.pallas_call(
        functools.partial(_mix_kernel, tiles_per_seq=tpb),
        grid=(t // tm, nn),
        in_specs=[pl.BlockSpec((tm, D_MODEL), lambda i, n: (i, 0)),
                  pl.BlockSpec((tm, D_MODEL), lambda i, n: (i, 0)),
                  _mod_spec(),
                  pl.BlockSpec((1, D_MODEL), lambda i, n: (0, 0)),
                  pl.BlockSpec((tm, ATTN_WIDTH), lambda i, n: (i, 0)),
                  pl.BlockSpec((None, tm, SSM_WIDTH), lambda i, n: (i // tpb, i % tpb, 0)),
                  pl.BlockSpec((D_MODEL, tn), lambda i, n: (0, n)),
                  pl.BlockSpec((D_MODEL, tn), lambda i, n: (0, nn + n)),
                  pl.BlockSpec((ATTN_WIDTH, tn), lambda i, n: (0, n)),
                  pl.BlockSpec((SSM_WIDTH, tn), lambda i, n: (0, n)),
                  pl.BlockSpec((SSM_WIDTH, tn), lambda i, n: (0, nn + n)),
                  pl.BlockSpec((tn, D_MODEL), lambda i, n: (n, 0))],
        out_specs=[pl.BlockSpec((tm, D_MODEL), lambda i, n: (i, 0)),
                   pl.BlockSpec((tm, D_MODEL), lambda i, n: (i, 0))],
        out_shape=[jax.ShapeDtypeStruct((t, D_MODEL), F32),
                   jax.ShapeDtypeStruct((t, D_MODEL), BF16)],
        compiler_params=_cparams(("arbitrary", "arbitrary"), 56),
        name="mix",
    )(x2, h, mod, g2n, o, gy3, wgates, wgates, wap, wglu, wglu, wout)


def _ffn_kernel(h_ref, halo_ref, x1_ref, mod_ref, wg_ref, wv_ref, cw_ref, cb_ref, wd_ref, gf_ref, out_ref,
                gate_scr, act_even, act_odd, *, tm, n_tiles, tiles_per_seq, final_norm):
    f = pl.program_id(1)
    first = (pl.program_id(0) % tiles_per_seq) == 0
    act_bufs = (act_even, act_odd)

    def down_stage(act_ref):
        out_ref[...] += jnp.dot(act_ref[...], wd_ref[...].astype(BF16), preferred_element_type=F32)

    def step(act_out, act_in):
        h = h_ref[...]
        wg = wg_ref[...].astype(BF16)
        gate = jnp.dot(h, wg, preferred_element_type=F32)
        val = jnp.dot(h, wv_ref[...].astype(BF16), preferred_element_type=F32)
        halo = jnp.dot(halo_ref[...], wg, preferred_element_type=F32)
        if act_in is not None:
            down_stage(act_in)
        ext = jnp.concatenate([jnp.where(first, 0.0, halo), gate], axis=0)
        g1 = pltpu.roll(ext, 1, 0)[V7X_SUBLANES:]
        g2 = pltpu.roll(ext, 2, 0)[V7X_SUBLANES:]
        conv = cw_ref[0:1, :] * g2 + cw_ref[1:2, :] * g1 + cw_ref[2:3, :] * gate + cb_ref[...]
        act_out[...] = (_silu(conv) * val).astype(BF16)

    @pl.when(f == 0)
    def _():
        out_ref[...] = jnp.zeros_like(out_ref)
        step(act_bufs[0], None)

    for parity in range(2):
        @pl.when((f > 0) & (f < n_tiles) & (f % 2 == parity))
        def _():
            step(act_bufs[parity], act_bufs[1 - parity])

    @pl.when(f == n_tiles)
    def _():
        down_stage(act_bufs[(n_tiles - 1) % 2])
        x2 = x1_ref[...] + _mod_row(mod_ref, 5, pl.program_id(0) // tiles_per_seq) * out_ref[...]
        out_ref[...] = _rms(x2) * gf_ref[...] if final_norm else x2


def _ffn(h2, x1, mod, wup, conv_w, conv_b, wdown, gf, seq, final_norm):
    t = h2.shape[0]
    tm, tf = 1024, 256
    tpb = seq // tm
    nf = D_FF // tf
    hb = tm // V7X_SUBLANES
    return pl.pallas_call(
        functools.partial(_ffn_kernel, tm=tm, n_tiles=nf, tiles_per_seq=tpb, final_norm=final_norm),
        grid=(t // tm, nf + 1),
        in_specs=[pl.BlockSpec((tm, D_MODEL), lambda i, f: (i, 0)),
                  pl.BlockSpec((V7X_SUBLANES, D_MODEL), lambda i, f: (jnp.maximum(i * hb - 1, 0), 0)),
                  pl.BlockSpec((tm, D_MODEL), lambda i, f: (i, 0), pipeline_mode=pl.Buffered(1)),
                  _mod_spec(),
                  pl.BlockSpec((D_MODEL, tf), lambda i, f: (0, jnp.minimum(f, nf - 1))),
                  pl.BlockSpec((D_MODEL, tf), lambda i, f: (0, nf + jnp.minimum(f, nf - 1))),
                  pl.BlockSpec((3, tf), lambda i, f: (0, jnp.minimum(f, nf - 1))),
                  pl.BlockSpec((1, tf), lambda i, f: (0, jnp.minimum(f, nf - 1))),
                  pl.BlockSpec((tf, D_MODEL), lambda i, f: (jnp.maximum(f - 1, 0), 0)),
                  pl.BlockSpec((1, D_MODEL), lambda i, f: (0, 0))],
        out_specs=pl.BlockSpec((tm, D_MODEL), lambda i, f: (i, 0)),
        out_shape=jax.ShapeDtypeStruct((t, D_MODEL), F32),
        scratch_shapes=[pltpu.VMEM((tm + V7X_SUBLANES, tf), F32), pltpu.VMEM((tm, tf), BF16), pltpu.VMEM((tm, tf), BF16)],
        compiler_params=_cparams(("arbitrary", "arbitrary"), 60),
        name="ffn",
    )(h2, h2, x1, mod, wup, wup, conv_w, conv_b, wdown, gf)


def kernel(x, c, ada_w, ada_b, norm_mix_g, w_in, attn_sinks, w_attn_proj, ssm_a_re, ssm_a_im, ssm_log_dt,
           ssm_b_re, ssm_b_im, ssm_c_re, ssm_c_im, ssm_d, w_ssm_glu, w_out, norm_ffn_g, w_ffn_up, ffn_conv_w,
           ffn_conv_b, w_ffn_down, final_g):
    bsz, seq, _ = x.shape
    x2 = x.reshape(bsz * seq, D_MODEL)
    for i in range(ada_w.shape[0]):
        mod = _ada(c, ada_w[i], ada_b[i].reshape(1, -1))

        win = w_in[i]
        g1n = norm_mix_g[i].reshape(1, D_MODEL)
        g2n = norm_ffn_g[i].reshape(1, D_MODEL)

        h, qkv, u, w_gates = _inproj(x2, mod, g1n, win, seq)
        o, w_ap, w_glu, w_o = _attention(attn_sinks[i], qkv, seq, (w_attn_proj[i], w_ssm_glu[i], w_out[i]))

        lam_r, lam_i, bw, cw = _ssmprep(
            ssm_a_re[i], ssm_a_im[i], ssm_log_dt[i].reshape(-1, 1),
            ssm_b_re[i].transpose(0, 2, 1), ssm_b_im[i].transpose(0, 2, 1),
            ssm_c_re[i].transpose(0, 2, 1), ssm_c_im[i].transpose(0, 2, 1))
        gy = _ssm(u.reshape(bsz, seq, SSM_WIDTH), bw, cw,
                  lam_r.reshape(STATE_ROWS, V7X_LANES), lam_i.reshape(STATE_ROWS, V7X_LANES),
                  ssm_d[i].reshape(1, SSM_WIDTH))

        x2, h2 = _mix(x2, h, mod, g2n, o, gy, w_gates, w_ap, w_glu, w_o, seq)
        x2 = _ffn(h2, x2, mod, w_ffn_up[i], ffn_conv_w[i], ffn_conv_b[i].reshape(1, D_FF),
                  w_ffn_down[i], final_g.reshape(1, D_MODEL), seq, i == ada_w.shape[0] - 1)
    return x2.reshape(bsz, seq, D_MODEL)
```

```python
import functools

import jax
import jax.numpy as jnp
import numpy as np
from jax import lax
from jax.experimental import pallas as pl
from jax.experimental.pallas import tpu as pltpu

F32 = jnp.float32
BF16 = jnp.bfloat16

D_MODEL = 2048
HEAD_DIM = 64
N_Q_HEADS = 16
N_KV_HEADS = 2
Q_PER_KV = N_Q_HEADS // N_KV_HEADS
ATTN_WIDTH = N_Q_HEADS * HEAD_DIM
KV_WIDTH = N_KV_HEADS * HEAD_DIM
QKV_WIDTH = ATTN_WIDTH + 2 * KV_WIDTH
WINDOW = 128
BLOCK = 128
NEG_INF = -1e30
SSM_WIDTH = D_MODEL // 2
SSM_GROUP = 16
N_SSM_GROUPS = SSM_WIDTH // SSM_GROUP
SSM_STATE = 64
N_STATES = N_SSM_GROUPS * SSM_STATE
D_FF = 5632
RMS_EPS = 1e-6

V7X_SUBLANES = 8
V7X_LANES = 128
STATE_ROWS = N_STATES // V7X_LANES
CH_BLOCKS = SSM_WIDTH // V7X_LANES
GROUPS_PER_BLOCK = V7X_LANES // SSM_GROUP
STATES_PER_BLOCK = GROUPS_PER_BLOCK * SSM_STATE
TILES_PER_BLOCK = STATES_PER_BLOCK // V7X_LANES


def _cparams(sem, vmem_mb):
    return pltpu.CompilerParams(dimension_semantics=sem, vmem_limit_bytes=vmem_mb * 1024 * 1024)


def _sigmoid(x):
    return 1.0 / (1.0 + jnp.exp(-x))


def _silu(x):
    return x * _sigmoid(x)


def _gelu_tanh(x):
    c = np.float32(np.sqrt(2.0 / np.pi))
    return x * (0.5 * (1.0 + jnp.tanh(c * (x + np.float32(0.044715) * (x * x * x)))))


def _rms(x):
    return x * lax.rsqrt(jnp.mean(x * x, axis=-1, keepdims=True) + RMS_EPS)


def _modulated_norm(x, g, scale, shift):
    return _rms(x) * (g * (1.0 + scale)) + shift


N_MOD = 6
MOD_ROWS = V7X_SUBLANES


def _ada_kernel(c_ref, w_ref, b_ref, o_ref):
    c = c_ref[...]
    cond = jnp.concatenate([_silu(c), jnp.zeros((MOD_ROWS - c.shape[0], c.shape[1]), F32)], axis=0)
    o_ref[...] = jnp.dot(cond.astype(BF16), w_ref[...].astype(BF16), preferred_element_type=F32) + b_ref[...]


def _ada(c, w, b):
    assert c.shape[0] <= MOD_ROWS and w.shape[1] == N_MOD * D_MODEL
    tn = 1024
    per = D_MODEL // tn
    return pl.pallas_call(
        _ada_kernel,
        grid=(N_MOD * per,),
        in_specs=[pl.BlockSpec(c.shape, lambda j: (0, 0)),
                  pl.BlockSpec((D_MODEL, tn), lambda j: (0, j)),
                  pl.BlockSpec((1, tn), lambda j: (0, j))],
        out_specs=pl.BlockSpec((None, MOD_ROWS, tn), lambda j: (j // per, 0, j % per)),
        out_shape=jax.ShapeDtypeStruct((N_MOD, MOD_ROWS, D_MODEL), F32),
        compiler_params=_cparams(("arbitrary",), 40),
        name="ada",
    )(c, w, b)


def _mod_spec():
    return pl.BlockSpec((N_MOD, MOD_ROWS, D_MODEL), lambda *_: (0, 0, 0))


def _mod_row(mod_ref, k, b):
    return mod_ref[k, pl.ds(b, 1), :]


def _inproj_kernel(x_ref, mod_ref, g_ref, w_ref, wgate_ref, h_ref, qkv_ref, u_ref, wgate_bf_ref, wb_scr,
                   *, tiles_per_seq):
    @pl.when(pl.program_id(0) == 0)
    def _():
        wb_scr[...] = w_ref[...].astype(BF16)

    wgate_bf_ref[...] = wgate_ref[...].astype(BF16)

    b = pl.program_id(0) // tiles_per_seq
    h = _modulated_norm(x_ref[...], g_ref[...], _mod_row(mod_ref, 1, b), _mod_row(mod_ref, 0, b)).astype(BF16)
    h_ref[...] = h
    p = jnp.dot(h, wb_scr[...], preferred_element_type=F32)
    qkv_ref[...] = p[:, :QKV_WIDTH].astype(BF16)
    u_ref[...] = p[:, QKV_WIDTH:]


def _inproj(x2, mod, g, w_in, seq):
    t = x2.shape[0]
    tm = 512
    tpb = seq // tm
    n = QKV_WIDTH + SSM_WIDTH
    steps = t // tm
    gate_cols = w_in.shape[1] - n
    gw = gate_cols // steps
    assert gate_cols % steps == 0 and n % gw == 0 and gw % V7X_LANES == 0
    return pl.pallas_call(
        functools.partial(_inproj_kernel, tiles_per_seq=tpb),
        grid=(steps,),
        in_specs=[pl.BlockSpec((tm, D_MODEL), lambda i: (i, 0)),
                  _mod_spec(),
                  pl.BlockSpec((1, D_MODEL), lambda i: (0, 0)),
                  pl.BlockSpec((D_MODEL, n), lambda i: (0, 0), pipeline_mode=pl.Buffered(1)),
                  pl.BlockSpec((D_MODEL, gw), lambda i: (0, n // gw + i))],
        out_specs=[pl.BlockSpec((tm, D_MODEL), lambda i: (i, 0)),
                   pl.BlockSpec((tm, QKV_WIDTH), lambda i: (i, 0)),
                   pl.BlockSpec((tm, SSM_WIDTH), lambda i: (i, 0)),
                   pl.BlockSpec((D_MODEL, gw), lambda i: (0, i))],
        out_shape=[jax.ShapeDtypeStruct((t, D_MODEL), BF16),
                   jax.ShapeDtypeStruct((t, QKV_WIDTH), BF16),
                   jax.ShapeDtypeStruct((t, SSM_WIDTH), F32),
                   jax.ShapeDtypeStruct((D_MODEL, gate_cols), BF16)],
        scratch_shapes=[pltpu.VMEM((D_MODEL, n), BF16)],
        compiler_params=_cparams(("arbitrary",), 60),
        name="inproj",
    )(x2, mod, g, w_in, w_in)


def _attn_kernel(sink_ref, q_ref, kvc_ref, kvp_ref, w0_ref, w1_ref, w2_ref, o_ref, w0b_ref, w1b_ref, w2b_ref,
                 s_scr, p_scr, t_scr, *, tq, tiles_per_seq):
    for src, dst in ((w0_ref, w0b_ref), (w1_ref, w1b_ref), (w2_ref, w2b_ref)):
        dst[...] = src[...].astype(BF16)

    first = (pl.program_id(0) % tiles_per_seq) == 0
    qi = lax.broadcasted_iota(jnp.int32, (BLOCK, BLOCK), 0)
    kj = lax.broadcasted_iota(jnp.int32, (BLOCK, BLOCK), 1)
    upper = kj > qi
    pairs = Q_PER_KV // 2
    pair_w = 2 * HEAD_DIM
    zeros = jnp.zeros((2 * BLOCK, HEAD_DIM), BF16)
    ones = jnp.ones((2 * BLOCK, pair_w), BF16)
    scale = jnp.asarray(HEAD_DIM ** -0.5, BF16)
    contract_lanes = (((1,), (1,)), ((), ()))
    n_qb = tq // BLOCK

    def window(qb):
        r0 = qb * BLOCK
        prev = kvp_ref[...] if qb == 0 else kvc_ref[r0 - BLOCK:r0, :]
        return jnp.concatenate([prev, kvc_ref[r0:r0 + BLOCK, :]], axis=0)

    def slot(qb, g, par):
        return (qb * N_KV_HEADS + g) * 2 + par

    for qb in range(n_qb):
        r0 = qb * BLOCK
        win = window(qb)
        for g in range(N_KV_HEADS):
            kg = win[:, g * HEAD_DIM:(g + 1) * HEAD_DIM]
            kz = (jnp.concatenate([kg, zeros], axis=1), jnp.concatenate([zeros, kg], axis=1))
            c0 = g * Q_PER_KV * HEAD_DIM
            qg = q_ref[r0:r0 + BLOCK, c0:c0 + Q_PER_KV * HEAD_DIM] * scale
            qs = jnp.concatenate([qg[:, p * pair_w:(p + 1) * pair_w] for p in range(pairs)], axis=0)
            for par in range(2):
                s_scr[slot(qb, g, par)] = lax.dot_general(qs, kz[par], contract_lanes, preferred_element_type=F32)

    for qb in range(n_qb):
        pad_bias = jnp.where(first, NEG_INF, 0.0).astype(F32) if qb == 0 else None
        for g in range(N_KV_HEADS):
            for par in range(2):
                sl = slot(qb, g, par)
                for p in range(pairs):
                    rows = slice(p * BLOCK, (p + 1) * BLOCK)
                    s_prev = s_scr[sl, rows, :BLOCK]
                    if pad_bias is not None:
                        s_prev = s_prev + pad_bias
                    sm = jnp.where(upper, s_prev, s_scr[sl, rows, BLOCK:])
                    sink = sink_ref[g * Q_PER_KV + 2 * p + par]
                    m = jnp.maximum(jnp.max(sm, axis=-1, keepdims=True), sink)
                    m = jnp.broadcast_to(m, (BLOCK, BLOCK))
                    pe = jnp.exp(sm - m)
                    p_scr[sl, rows, :BLOCK] = jnp.where(upper, pe, 0.0).astype(BF16)
                    p_scr[sl, rows, BLOCK:] = jnp.where(upper, 0.0, pe).astype(BF16)
                    t_scr[sl, rows, :] = jnp.exp(sink - m)

    for qb in range(n_qb):
        r0 = qb * BLOCK
        win = window(qb)
        for g in range(N_KV_HEADS):
            vg = win[:, KV_WIDTH + g * HEAD_DIM:KV_WIDTH + (g + 1) * HEAD_DIM]
            vz = (jnp.concatenate([vg, zeros, ones], axis=1), jnp.concatenate([zeros, vg, ones], axis=1))
            c0 = g * Q_PER_KV * HEAD_DIM
            acc = None
            for par in range(2):
                sl = slot(qb, g, par)
                res = jnp.dot(p_scr[sl], vz[par], preferred_element_type=F32)
                on = res[:, :pair_w] / (res[:, pair_w:] + t_scr[sl])
                acc = on if acc is None else acc + on
            for p in range(pairs):
                o_ref[r0:r0 + BLOCK, c0 + p * pair_w:c0 + (p + 1) * pair_w] = (
                    acc[p * BLOCK:(p + 1) * BLOCK].astype(BF16))


def _attention(sinks, qkv, seq, weights):
    t = qkv.shape[0]
    tq = 512
    tiles_per_seq = seq // tq
    kv_col = ATTN_WIDTH // (2 * KV_WIDTH)
    rb = tq // BLOCK
    steps = t // tq
    slots = rb * N_KV_HEADS * 2
    rows = (Q_PER_KV // 2) * BLOCK
    for w in weights:
        assert w.shape[1] % (steps * V7X_LANES) == 0
    w_specs = [pl.BlockSpec((w.shape[0], w.shape[1] // steps), lambda i: (0, i)) for w in weights]
    return pl.pallas_call(
        functools.partial(_attn_kernel, tq=tq, tiles_per_seq=tiles_per_seq),
        grid=(steps,),
        in_specs=[pl.BlockSpec(memory_space=pltpu.SMEM),
                  pl.BlockSpec((tq, ATTN_WIDTH), lambda i: (i, 0)),
                  pl.BlockSpec((tq, 2 * KV_WIDTH), lambda i: (i, kv_col)),
                  pl.BlockSpec((BLOCK, 2 * KV_WIDTH), lambda i: (jnp.maximum(i * rb - 1, 0), kv_col))] + w_specs,
        out_specs=[pl.BlockSpec((tq, ATTN_WIDTH), lambda i: (i, 0))] + w_specs,
        out_shape=[jax.ShapeDtypeStruct((t, ATTN_WIDTH), BF16)]
                  + [jax.ShapeDtypeStruct(w.shape, BF16) for w in weights],
        scratch_shapes=[pltpu.VMEM((slots, rows, 2 * BLOCK), F32),
                        pltpu.VMEM((slots, rows, 2 * BLOCK), BF16),
                        pltpu.VMEM((slots, rows, BLOCK), F32)],
        compiler_params=_cparams(("arbitrary",), 48),
        name="attn",
    )(sinks, qkv, qkv, qkv, *weights)


def _ssmprep_kernel(ar_ref, ai_ref, ldt_ref, br_ref, bi_ref, cr_ref, ci_ref, lr_ref, li_ref, bw_ref, cw_ref):
    ar = ar_ref[...]
    ai = ai_ref[...]
    dt = jnp.exp(ldt_ref[...])
    mag = jnp.exp(ar * dt)
    lr = mag * jnp.cos(ai * dt)
    li = mag * jnp.sin(ai * dt)
    den = ar * ar + ai * ai
    zr = ((lr - 1.0) * ar + li * ai) / den
    zi = (li * ar - (lr - 1.0) * ai) / den
    lr_ref[...] = lr
    li_ref[...] = li

    def same_group(shape, row_w, col_w):
        rows = lax.broadcasted_iota(jnp.int32, shape, 0) // row_w
        cols = lax.broadcasted_iota(jnp.int32, shape, 1) // col_w
        return rows == cols

    b_mask = same_group((V7X_LANES, STATES_PER_BLOCK), SSM_GROUP, SSM_STATE)
    c_mask = same_group((STATES_PER_BLOCK, V7X_LANES), SSM_STATE, SSM_GROUP)
    for cb in range(CH_BLOCKS):
        gs = slice(cb * GROUPS_PER_BLOCK, (cb + 1) * GROUPS_PER_BLOCK)
        zr_b = zr[gs][:, None, :]
        zi_b = zi[gs][:, None, :]
        br = br_ref[gs]
        bi = bi_ref[gs]
        for part, bbar in enumerate((zr_b * br - zi_b * bi, zr_b * bi + zi_b * br)):
            rows = bbar.reshape(V7X_LANES, SSM_STATE)
            tiled = jnp.concatenate([rows] * GROUPS_PER_BLOCK, axis=1)
            bw_ref[cb, :, part * STATES_PER_BLOCK:(part + 1) * STATES_PER_BLOCK] = (
                jnp.where(b_mask, tiled, 0.0).astype(BF16))
        for part, c_ref in enumerate((cr_ref, ci_ref)):
            rows = c_ref[gs].reshape(STATES_PER_BLOCK, SSM_GROUP)
            tiled = jnp.concatenate([rows] * GROUPS_PER_BLOCK, axis=1)
            signed = tiled if part == 0 else -tiled
            cw_ref[cb, part * STATES_PER_BLOCK:(part + 1) * STATES_PER_BLOCK, :] = (
                jnp.where(c_mask, signed, 0.0).astype(BF16))


def _ssmprep(a_re, a_im, log_dt, b_re_t, b_im_t, c_re_t, c_im_t):
    g, n = a_re.shape
    return pl.pallas_call(
        _ssmprep_kernel,
        out_shape=[jax.ShapeDtypeStruct((g, n), F32), jax.ShapeDtypeStruct((g, n), F32),
                   jax.ShapeDtypeStruct((CH_BLOCKS, V7X_LANES, 2 * STATES_PER_BLOCK), BF16),
                   jax.ShapeDtypeStruct((CH_BLOCKS, 2 * STATES_PER_BLOCK, V7X_LANES), BF16)],
        name="ssmprep",
    )(a_re, a_im, log_dt, b_re_t, b_im_t, c_re_t, c_im_t)


SLAB_PITCH = 2 * STATE_ROWS + 4


def _ssm_kernel(u_ref, bw_ref, cw_ref, lr_ref, li_ref, d_ref, gy_ref, hcr_ref, hci_ref, *bufs, tc, nb):
    tiles = tc // V7X_SUBLANES

    @pl.when(pl.program_id(0) == 0)
    def _():
        hcr_ref[...] = jnp.zeros_like(hcr_ref)
        hci_ref[...] = jnp.zeros_like(hci_ref)

    def tile_rows(ti, state_tile, imag):
        return pl.ds(ti * V7X_SUBLANES * SLAB_PITCH + 2 * state_tile + imag, V7X_SUBLANES, stride=SLAB_PITCH)

    def slab_rows(t, imag):
        return pl.ds(t * SLAB_PITCH + imag, STATE_ROWS, stride=2)

    def project_in(b):
        x_ref = bufs[2 * b]
        ub = u_ref[b].astype(BF16)
        for cb in range(CH_BLOCKS):
            res = jnp.dot(ub[:, cb * V7X_LANES:(cb + 1) * V7X_LANES], bw_ref[cb], preferred_element_type=F32)
            for k in range(TILES_PER_BLOCK):
                for ti in range(tiles):
                    rs = slice(ti * V7X_SUBLANES, (ti + 1) * V7X_SUBLANES)
                    st = cb * TILES_PER_BLOCK + k
                    x_ref[tile_rows(ti, st, 0), :] = res[rs, k * V7X_LANES:(k + 1) * V7X_LANES]
                    x_ref[tile_rows(ti, st, 1), :] = res[rs, STATES_PER_BLOCK + k * V7X_LANES:
                                                         STATES_PER_BLOCK + (k + 1) * V7X_LANES]

    def recur(b):
        x_ref, h_ref = bufs[2 * b], bufs[2 * b + 1]
        lr = lr_ref[...]
        li = li_ref[...]
        hr = hcr_ref[b]
        hi = hci_ref[b]
        for t in range(tc):
            nr = lr * hr - li * hi + x_ref[slab_rows(t, 0), :]
            ni = lr * hi + li * hr + x_ref[slab_rows(t, 1), :]
            h_ref[slab_rows(t, 0), :] = nr
            h_ref[slab_rows(t, 1), :] = ni
            hr, hi = nr, ni
        hcr_ref[b] = hr
        hci_ref[b] = hi

    def project_out(b):
        h_ref = bufs[2 * b + 1]
        u = u_ref[b]
        for cb in range(CH_BLOCKS):
            row_tiles = []
            for ti in range(tiles):
                parts = [h_ref[tile_rows(ti, cb * TILES_PER_BLOCK + k, imag), :]
                         for imag in (0, 1) for k in range(TILES_PER_BLOCK)]
                row_tiles.append(jnp.concatenate(parts, axis=-1))
            hcat = jnp.concatenate(row_tiles, axis=0).astype(BF16)
            y = jnp.dot(hcat, cw_ref[cb], preferred_element_type=F32)
            cols = slice(cb * V7X_LANES, (cb + 1) * V7X_LANES)
            y = y + d_ref[:, cols] * u[:, cols]
            gy_ref[b, :, cols] = _gelu_tanh(y).astype(BF16)

    for b in range(nb):
        project_in(b)
    for b in range(nb):
        recur(b)
        project_out(b)


def _ssm(u3, bw, cw, lam_r, lam_i, dskip):
    nb, seq, _ = u3.shape
    tc = 256
    slab_buf = pltpu.VMEM((tc * SLAB_PITCH, V7X_LANES), F32)
    return pl.pallas_call(
        functools.partial(_ssm_kernel, tc=tc, nb=nb),
        grid=(seq // tc,),
        in_specs=[pl.BlockSpec((nb, tc, SSM_WIDTH), lambda j: (0, j, 0)),
                  pl.BlockSpec(bw.shape, lambda j: (0, 0, 0)),
                  pl.BlockSpec(cw.shape, lambda j: (0, 0, 0)),
                  pl.BlockSpec((STATE_ROWS, V7X_LANES), lambda j: (0, 0)),
                  pl.BlockSpec((STATE_ROWS, V7X_LANES), lambda j: (0, 0)),
                  pl.BlockSpec((1, SSM_WIDTH), lambda j: (0, 0))],
        out_specs=pl.BlockSpec((nb, tc, SSM_WIDTH), lambda j: (0, j, 0)),
        out_shape=jax.ShapeDtypeStruct((nb, seq, SSM_WIDTH), BF16),
        scratch_shapes=[pltpu.VMEM((nb, STATE_ROWS, V7X_LANES), F32),
                        pltpu.VMEM((nb, STATE_ROWS, V7X_LANES), F32)] + [slab_buf] * (2 * nb),
        compiler_params=_cparams(("arbitrary",), 56),
        name="ssm",
    )(u3, bw, cw, lam_r, lam_i, dskip)


def _mix_kernel(x_ref, h_ref, mod_ref, g2n_ref, o_ref, gy_ref, wga_ref, wgs_ref, wap_ref, wla_ref, wlb_ref,
                wout_ref, x1_ref, h2_ref, *, tiles_per_seq):
    n = pl.program_id(1)
    b = pl.program_id(0) // tiles_per_seq

    @pl.when(n == 0)
    def _():
        x1_ref[...] = jnp.zeros_like(x1_ref)

    h = h_ref[...]
    ga = jnp.dot(h, wga_ref[...], preferred_element_type=F32)
    gs = jnp.dot(h, wgs_ref[...], preferred_element_type=F32)
    at = jnp.dot(o_ref[...], wap_ref[...], preferred_element_type=F32)
    gy = gy_ref[...]
    la = jnp.dot(gy, wla_ref[...], preferred_element_type=F32)
    lb = jnp.dot(gy, wlb_ref[...], preferred_element_type=F32)
    ssm = la * _sigmoid(lb)
    mixed = _sigmoid(ga) * at + _sigmoid(gs) * ssm
    x1_ref[...] += jnp.dot(mixed.astype(BF16), wout_ref[...], preferred_element_type=F32)

    @pl.when(n == pl.num_programs(1) - 1)
    def _():
        x1 = x_ref[...] + _mod_row(mod_ref, 2, b) * x1_ref[...]
        x1_ref[...] = x1
        h2_ref[...] = _modulated_norm(x1, g2n_ref[...], _mod_row(mod_ref, 4, b), _mod_row(mod_ref, 3, b)).astype(BF16)


def _mix(x2, h, mod, g2n, o, gy3, wgates, wap, wglu, wout, seq):
    t = x2.shape[0]
    tm, tn = 512, 512
    tpb = seq // tm
    nn = D_MODEL // tn
    return pl.pallas_call(
        functools.partial(_mix_kernel, tiles_per_seq=tpb),
        grid=(t // tm, nn),
        in_specs=[pl.BlockSpec((tm, D_MODEL), lambda i, n: (i, 0)),
                  pl.BlockSpec((tm, D_MODEL), lambda i, n: (i, 0)),
                  _mod_spec(),
                  pl.BlockSpec((1, D_MODEL), lambda i, n: (0, 0)),
                  pl.BlockSpec((tm, ATTN_WIDTH), lambda i, n: (i, 0)),
                  pl.BlockSpec((None, tm, SSM_WIDTH), lambda i, n: (i // tpb, i % tpb, 0)),
                  pl.BlockSpec((D_MODEL, tn), lambda i, n: (0, n)),
                  pl.BlockSpec((D_MODEL, tn), lambda i, n: (0, nn + n)),
                  pl.BlockSpec((ATTN_WIDTH, tn), lambda i, n: (0, n)),
                  pl.BlockSpec((SSM_WIDTH, tn), lambda i, n: (0, n)),
                  pl.BlockSpec((SSM_WIDTH, tn), lambda i, n: (0, nn + n)),
                  pl.BlockSpec((tn, D_MODEL), lambda i, n: (n, 0))],
        out_specs=[pl.BlockSpec((tm, D_MODEL), lambda i, n: (i, 0)),
                   pl.BlockSpec((tm, D_MODEL), lambda i, n: (i, 0))],
        out_shape=[jax.ShapeDtypeStruct((t, D_MODEL), F32),
                   jax.ShapeDtypeStruct((t, D_MODEL), BF16)],
        compiler_params=_cparams(("arbitrary", "arbitrary"), 56),
        name="mix",
    )(x2, h, mod, g2n, o, gy3, wgates, wgates, wap, wglu, wglu, wout)


def _ffn_kernel(h_ref, halo_ref, x1_ref, mod_ref, wg_ref, wv_ref, cw_ref, cb_ref, wd_ref, gf_ref, out_ref,
                act_even, act_odd, *, tm, n_tiles, tiles_per_seq, final_norm):
    f = pl.program_id(1)
    first = (pl.program_id(0) % tiles_per_seq) == 0
    act_bufs = (act_even, act_odd)

    def down_stage(act_ref):
        out_ref[...] += jnp.dot(act_ref[...], wd_ref[...].astype(BF16), preferred_element_type=F32)

    def step(act_out, act_in):
        h = h_ref[...]
        wg = wg_ref[...].astype(BF16)
        gate = jnp.dot(h, wg, preferred_element_type=F32)
        val = jnp.dot(h, wv_ref[...].astype(BF16), preferred_element_type=F32)
        halo = jnp.dot(halo_ref[...], wg, preferred_element_type=F32)
        if act_in is not None:
            down_stage(act_in)
        ext = jnp.concatenate([jnp.where(first, 0.0, halo), gate], axis=0)
        g1 = pltpu.roll(ext, 1, 0)[V7X_SUBLANES:]
        g2 = pltpu.roll(ext, 2, 0)[V7X_SUBLANES:]
        conv = cw_ref[0:1, :] * g2 + cw_ref[1:2, :] * g1 + cw_ref[2:3, :] * gate + cb_ref[...]
        act_out[...] = (_silu(conv) * val).astype(BF16)

    @pl.when(f == 0)
    def _():
        out_ref[...] = jnp.zeros_like(out_ref)
        step(act_bufs[0], None)

    for parity in range(2):
        @pl.when((f > 0) & (f < n_tiles) & (f % 2 == parity))
        def _():
            step(act_bufs[parity], act_bufs[1 - parity])

    @pl.when(f == n_tiles)
    def _():
        down_stage(act_bufs[(n_tiles - 1) % 2])
        x2 = x1_ref[...] + _mod_row(mod_ref, 5, pl.program_id(0) // tiles_per_seq) * out_ref[...]
        out_ref[...] = _rms(x2) * gf_ref[...] if final_norm else x2


def _ffn(h2, x1, mod, wup, conv_w, conv_b, wdown, gf, seq, final_norm):
    t = h2.shape[0]
    tm, tf = 1024, 256
    tpb = seq // tm
    nf = D_FF // tf
    hb = tm // V7X_SUBLANES
    return pl.pallas_call(
        functools.partial(_ffn_kernel, tm=tm, n_tiles=nf, tiles_per_seq=tpb, final_norm=final_norm),
        grid=(t // tm, nf + 1),
        in_specs=[pl.BlockSpec((tm, D_MODEL), lambda i, f: (i, 0)),
                  pl.BlockSpec((V7X_SUBLANES, D_MODEL), lambda i, f: (jnp.maximum(i * hb - 1, 0), 0)),
                  pl.BlockSpec((tm, D_MODEL), lambda i, f: (jnp.where(f >= 4, i, jnp.maximum(i - 1, 0)), 0)),
                  _mod_spec(),
                  pl.BlockSpec((D_MODEL, tf), lambda i, f: (0, jnp.minimum(f, nf - 1))),
                  pl.BlockSpec((D_MODEL, tf), lambda i, f: (0, nf + jnp.minimum(f, nf - 1))),
                  pl.BlockSpec((3, tf), lambda i, f: (0, jnp.minimum(f, nf - 1))),
                  pl.BlockSpec((1, tf), lambda i, f: (0, jnp.minimum(f, nf - 1))),
                  pl.BlockSpec((tf, D_MODEL), lambda i, f: (jnp.maximum(f - 1, 0), 0)),
                  pl.BlockSpec((1, D_MODEL), lambda i, f: (0, 0))],
        out_specs=pl.BlockSpec((tm, D_MODEL), lambda i, f: (i, 0)),
        out_shape=jax.ShapeDtypeStruct((t, D_MODEL), F32),
        scratch_shapes=[pltpu.VMEM((tm, tf), BF16), pltpu.VMEM((tm, tf), BF16)],
        compiler_params=_cparams(("arbitrary", "arbitrary"), 60),
        name="ffn",
    )(h2, h2, x1, mod, wup, wup, conv_w, conv_b, wdown, gf)


def kernel(x, c, ada_w, ada_b, norm_mix_g, w_in, attn_sinks, w_attn_proj, ssm_a_re, ssm_a_im, ssm_log_dt,
           ssm_b_re, ssm_b_im, ssm_c_re, ssm_c_im, ssm_d, w_ssm_glu, w_out, norm_ffn_g, w_ffn_up, ffn_conv_w,
           ffn_conv_b, w_ffn_down, final_g):
    bsz, seq, _ = x.shape
    x2 = x.reshape(bsz * seq, D_MODEL)
    for i in range(ada_w.shape[0]):
        mod = _ada(c, ada_w[i], ada_b[i].reshape(1, -1))

        win = w_in[i]
        g1n = norm_mix_g[i].reshape(1, D_MODEL)
        g2n = norm_ffn_g[i].reshape(1, D_MODEL)

        h, qkv, u, w_gates = _inproj(x2, mod, g1n, win, seq)
        o, w_ap, w_glu, w_o = _attention(attn_sinks[i], qkv, seq, (w_attn_proj[i], w_ssm_glu[i], w_out[i]))

        lam_r, lam_i, bw, cw = _ssmprep(
            ssm_a_re[i], ssm_a_im[i], ssm_log_dt[i].reshape(-1, 1),
            ssm_b_re[i].transpose(0, 2, 1), ssm_b_im[i].transpose(0, 2, 1),
            ssm_c_re[i].transpose(0, 2, 1), ssm_c_im[i].transpose(0, 2, 1))
        gy = _ssm(u.reshape(bsz, seq, SSM_WIDTH), bw, cw,
                  lam_r.reshape(STATE_ROWS, V7X_LANES), lam_i.reshape(STATE_ROWS, V7X_LANES),
                  ssm_d[i].reshape(1, SSM_WIDTH))

        x2, h2 = _mix(x2, h, mod, g2n, o, gy, w_gates, w_ap, w_glu, w_o, seq)
        x2 = _ffn(h2, x2, mod, w_ffn_up[i], ffn_conv_w[i], ffn_conv_b[i].reshape(1, D_FF),
                  w_ffn_down[i], final_g.reshape(1, D_MODEL), seq, i == ada_w.shape[0] - 1)
    return x2.reshape(bsz, seq, D_MODEL)
```

```python
import functools

import jax
import jax.numpy as jnp
import numpy as np
from jax import lax
from jax.experimental import pallas as pl
from jax.experimental.pallas import tpu as pltpu

F32 = jnp.float32
BF16 = jnp.bfloat16

D_MODEL = 2048
HEAD_DIM = 64
N_Q_HEADS = 16
N_KV_HEADS = 2
Q_PER_KV = N_Q_HEADS // N_KV_HEADS
ATTN_WIDTH = N_Q_HEADS * HEAD_DIM
KV_WIDTH = N_KV_HEADS * HEAD_DIM
QKV_WIDTH = ATTN_WIDTH + 2 * KV_WIDTH
WINDOW = 128
BLOCK = 128
NEG_INF = -1e30
SSM_WIDTH = D_MODEL // 2
SSM_GROUP = 16
N_SSM_GROUPS = SSM_WIDTH // SSM_GROUP
SSM_STATE = 64
N_STATES = N_SSM_GROUPS * SSM_STATE
D_FF = 5632
RMS_EPS = 1e-6

V7X_SUBLANES = 8
V7X_LANES = 128
STATE_ROWS = N_STATES // V7X_LANES
CH_BLOCKS = SSM_WIDTH // V7X_LANES
GROUPS_PER_BLOCK = V7X_LANES // SSM_GROUP
STATES_PER_BLOCK = GROUPS_PER_BLOCK * SSM_STATE
TILES_PER_BLOCK = STATES_PER_BLOCK // V7X_LANES


def _cparams(sem, vmem_mb):
    return pltpu.CompilerParams(dimension_semantics=sem, vmem_limit_bytes=vmem_mb * 1024 * 1024)


def _sigmoid(x):
    return 1.0 / (1.0 + jnp.exp(-x))


def _silu(x):
    return x * _sigmoid(x)


def _gelu_tanh(x):
    c = np.float32(np.sqrt(2.0 / np.pi))
    return x * (0.5 * (1.0 + jnp.tanh(c * (x + np.float32(0.044715) * (x * x * x)))))


def _rms(x):
    return x * lax.rsqrt(jnp.mean(x * x, axis=-1, keepdims=True) + RMS_EPS)


def _modulated_norm(x, g, scale, shift):
    return _rms(x) * (g * (1.0 + scale)) + shift


N_MOD = 6
MOD_ROWS = V7X_SUBLANES


def _ada_kernel(c_ref, w_ref, b_ref, o_ref):
    c = c_ref[...]
    cond = jnp.concatenate([_silu(c), jnp.zeros((MOD_ROWS - c.shape[0], c.shape[1]), F32)], axis=0)
    o_ref[...] = jnp.dot(cond.astype(BF16), w_ref[...].astype(BF16), preferred_element_type=F32) + b_ref[...]


def _ada(c, w, b):
    assert c.shape[0] <= MOD_ROWS and w.shape[1] == N_MOD * D_MODEL
    tn = 1024
    per = D_MODEL // tn
    return pl.pallas_call(
        _ada_kernel,
        grid=(N_MOD * per,),
        in_specs=[pl.BlockSpec(c.shape, lambda j: (0, 0)),
                  pl.BlockSpec((D_MODEL, tn), lambda j: (0, j)),
                  pl.BlockSpec((1, tn), lambda j: (0, j))],
        out_specs=pl.BlockSpec((None, MOD_ROWS, tn), lambda j: (j // per, 0, j % per)),
        out_shape=jax.ShapeDtypeStruct((N_MOD, MOD_ROWS, D_MODEL), F32),
        compiler_params=_cparams(("arbitrary",), 40),
        name="ada",
    )(c, w, b)


def _mod_spec():
    return pl.BlockSpec((N_MOD, MOD_ROWS, D_MODEL), lambda *_: (0, 0, 0))


def _mod_row(mod_ref, k, b):
    return mod_ref[k, pl.ds(b, 1), :]


def _inproj_kernel(x_ref, mod_ref, g_ref, w_ref, wgate_ref, h_ref, qkv_ref, u_ref, wgate_bf_ref, wb_scr,
                   *, tiles_per_seq):
    @pl.when(pl.program_id(0) == 0)
    def _():
        wb_scr[...] = w_ref[...].astype(BF16)

    wgate_bf_ref[...] = wgate_ref[...].astype(BF16)

    b = pl.program_id(0) // tiles_per_seq
    h = _modulated_norm(x_ref[...], g_ref[...], _mod_row(mod_ref, 1, b), _mod_row(mod_ref, 0, b)).astype(BF16)
    h_ref[...] = h
    p = jnp.dot(h, wb_scr[...], preferred_element_type=F32)
    qkv_ref[...] = p[:, :QKV_WIDTH].astype(BF16)
    u_ref[...] = p[:, QKV_WIDTH:]


def _inproj(x2, mod, g, w_in, seq):
    t = x2.shape[0]
    tm = 512
    tpb = seq // tm
    n = QKV_WIDTH + SSM_WIDTH
    steps = t // tm
    gate_cols = w_in.shape[1] - n
    gw = gate_cols // steps
    assert gate_cols % steps == 0 and n % gw == 0 and gw % V7X_LANES == 0
    return pl.pallas_call(
        functools.partial(_inproj_kernel, tiles_per_seq=tpb),
        grid=(steps,),
        in_specs=[pl.BlockSpec((tm, D_MODEL), lambda i: (i, 0)),
                  _mod_spec(),
                  pl.BlockSpec((1, D_MODEL), lambda i: (0, 0)),
                  pl.BlockSpec((D_MODEL, n), lambda i: (0, 0), pipeline_mode=pl.Buffered(1)),
                  pl.BlockSpec((D_MODEL, gw), lambda i: (0, n // gw + i))],
        out_specs=[pl.BlockSpec((tm, D_MODEL), lambda i: (i, 0)),
                   pl.BlockSpec((tm, QKV_WIDTH), lambda i: (i, 0)),
                   pl.BlockSpec((tm, SSM_WIDTH), lambda i: (i, 0)),
                   pl.BlockSpec((D_MODEL, gw), lambda i: (0, i))],
        out_shape=[jax.ShapeDtypeStruct((t, D_MODEL), BF16),
                   jax.ShapeDtypeStruct((t, QKV_WIDTH), BF16),
                   jax.ShapeDtypeStruct((t, SSM_WIDTH), F32),
                   jax.ShapeDtypeStruct((D_MODEL, gate_cols), BF16)],
        scratch_shapes=[pltpu.VMEM((D_MODEL, n), BF16)],
        compiler_params=_cparams(("arbitrary",), 60),
        name="inproj",
    )(x2, mod, g, w_in, w_in)


def _attn_kernel(sink_ref, q_ref, kvc_ref, kvp_ref, w0_ref, w1_ref, w2_ref, o_ref, w0b_ref, w1b_ref, w2b_ref,
                 s_scr, p_scr, t_scr, *, tq, tiles_per_seq):
    for src, dst in ((w0_ref, w0b_ref), (w1_ref, w1b_ref), (w2_ref, w2b_ref)):
        dst[...] = src[...].astype(BF16)

    first = (pl.program_id(0) % tiles_per_seq) == 0
    qi = lax.broadcasted_iota(jnp.int32, (BLOCK, BLOCK), 0)
    kj = lax.broadcasted_iota(jnp.int32, (BLOCK, BLOCK), 1)
    upper = kj > qi
    pairs = Q_PER_KV // 2
    pair_w = 2 * HEAD_DIM
    zeros = jnp.zeros((2 * BLOCK, HEAD_DIM), BF16)
    ones = jnp.ones((2 * BLOCK, pair_w), BF16)
    scale = jnp.asarray(HEAD_DIM ** -0.5, BF16)
    contract_lanes = (((1,), (1,)), ((), ()))
    n_qb = tq // BLOCK

    def window(qb):
        r0 = qb * BLOCK
        prev = kvp_ref[...] if qb == 0 else kvc_ref[r0 - BLOCK:r0, :]
        return jnp.concatenate([prev, kvc_ref[r0:r0 + BLOCK, :]], axis=0)

    def slot(qb, g, par):
        return (qb * N_KV_HEADS + g) * 2 + par

    for qb in range(n_qb):
        r0 = qb * BLOCK
        win = window(qb)
        for g in range(N_KV_HEADS):
            kg = win[:, g * HEAD_DIM:(g + 1) * HEAD_DIM]
            kz = (jnp.concatenate([kg, zeros], axis=1), jnp.concatenate([zeros, kg], axis=1))
            c0 = g * Q_PER_KV * HEAD_DIM
            qg = q_ref[r0:r0 + BLOCK, c0:c0 + Q_PER_KV * HEAD_DIM] * scale
            qs = jnp.concatenate([qg[:, p * pair_w:(p + 1) * pair_w] for p in range(pairs)], axis=0)
            for par in range(2):
                s_scr[slot(qb, g, par)] = lax.dot_general(qs, kz[par], contract_lanes, preferred_element_type=F32)

    for qb in range(n_qb):
        pad_bias = jnp.where(first, NEG_INF, 0.0).astype(F32) if qb == 0 else None
        for g in range(N_KV_HEADS):
            for par in range(2):
                sl = slot(qb, g, par)
                for p in range(pairs):
                    rows = slice(p * BLOCK, (p + 1) * BLOCK)
                    s_prev = s_scr[sl, rows, :BLOCK]
                    if pad_bias is not None:
                        s_prev = s_prev + pad_bias
                    sm = jnp.where(upper, s_prev, s_scr[sl, rows, BLOCK:])
                    sink = sink_ref[g * Q_PER_KV + 2 * p + par]
                    m = jnp.maximum(jnp.max(sm, axis=-1, keepdims=True), sink)
                    m = jnp.broadcast_to(m, (BLOCK, BLOCK))
                    pe = jnp.exp(sm - m)
                    p_scr[sl, rows, :BLOCK] = jnp.where(upper, pe, 0.0).astype(BF16)
                    p_scr[sl, rows, BLOCK:] = jnp.where(upper, 0.0, pe).astype(BF16)
                    t_scr[sl, rows, :] = jnp.exp(sink - m)

    for qb in range(n_qb):
        r0 = qb * BLOCK
        win = window(qb)
        for g in range(N_KV_HEADS):
            vg = win[:, KV_WIDTH + g * HEAD_DIM:KV_WIDTH + (g + 1) * HEAD_DIM]
            vz = (jnp.concatenate([vg, zeros, ones], axis=1), jnp.concatenate([zeros, vg, ones], axis=1))
            c0 = g * Q_PER_KV * HEAD_DIM
            acc = None
            for par in range(2):
                sl = slot(qb, g, par)
                res = jnp.dot(p_scr[sl], vz[par], preferred_element_type=F32)
                on = res[:, :pair_w] / (res[:, pair_w:] + t_scr[sl])
                acc = on if acc is None else acc + on
            for p in range(pairs):
                o_ref[r0:r0 + BLOCK, c0 + p * pair_w:c0 + (p + 1) * pair_w] = (
                    acc[p * BLOCK:(p + 1) * BLOCK].astype(BF16))


def _attention(sinks, qkv, seq, weights):
    t = qkv.shape[0]
    tq = 512
    tiles_per_seq = seq // tq
    kv_col = ATTN_WIDTH // (2 * KV_WIDTH)
    rb = tq // BLOCK
    steps = t // tq
    slots = rb * N_KV_HEADS * 2
    rows = (Q_PER_KV // 2) * BLOCK
    for w in weights:
        assert w.shape[1] % (steps * V7X_LANES) == 0
    w_specs = [pl.BlockSpec((w.shape[0], w.shape[1] // steps), lambda i: (0, i)) for w in weights]
    return pl.pallas_call(
        functools.partial(_attn_kernel, tq=tq, tiles_per_seq=tiles_per_seq),
        grid=(steps,),
        in_specs=[pl.BlockSpec(memory_space=pltpu.SMEM),
                  pl.BlockSpec((tq, ATTN_WIDTH), lambda i: (i, 0)),
                  pl.BlockSpec((tq, 2 * KV_WIDTH), lambda i: (i, kv_col)),
                  pl.BlockSpec((BLOCK, 2 * KV_WIDTH), lambda i: (jnp.maximum(i * rb - 1, 0), kv_col))] + w_specs,
        out_specs=[pl.BlockSpec((tq, ATTN_WIDTH), lambda i: (i, 0))] + w_specs,
        out_shape=[jax.ShapeDtypeStruct((t, ATTN_WIDTH), BF16)]
                  + [jax.ShapeDtypeStruct(w.shape, BF16) for w in weights],
        scratch_shapes=[pltpu.VMEM((slots, rows, 2 * BLOCK), F32),
                        pltpu.VMEM((slots, rows, 2 * BLOCK), BF16),
                        pltpu.VMEM((slots, rows, BLOCK), F32)],
        compiler_params=_cparams(("arbitrary",), 48),
        name="attn",
    )(sinks, qkv, qkv, qkv, *weights)


def _ssmprep_kernel(ar_ref, ai_ref, ldt_ref, br_ref, bi_ref, cr_ref, ci_ref, lr_ref, li_ref, bw_ref, cw_ref):
    ar = ar_ref[...]
    ai = ai_ref[...]
    dt = jnp.exp(ldt_ref[...])
    mag = jnp.exp(ar * dt)
    lr = mag * jnp.cos(ai * dt)
    li = mag * jnp.sin(ai * dt)
    den = ar * ar + ai * ai
    zr = ((lr - 1.0) * ar + li * ai) / den
    zi = (li * ar - (lr - 1.0) * ai) / den
    lr_ref[...] = lr
    li_ref[...] = li

    def same_group(shape, row_w, col_w):
        rows = lax.broadcasted_iota(jnp.int32, shape, 0) // row_w
        cols = lax.broadcasted_iota(jnp.int32, shape, 1) // col_w
        return rows == cols

    b_mask = same_group((V7X_LANES, STATES_PER_BLOCK), SSM_GROUP, SSM_STATE)
    c_mask = same_group((STATES_PER_BLOCK, V7X_LANES), SSM_STATE, SSM_GROUP)
    for cb in range(CH_BLOCKS):
        gs = slice(cb * GROUPS_PER_BLOCK, (cb + 1) * GROUPS_PER_BLOCK)
        zr_b = zr[gs][:, None, :]
        zi_b = zi[gs][:, None, :]
        br = br_ref[gs]
        bi = bi_ref[gs]
        for part, bbar in enumerate((zr_b * br - zi_b * bi, zr_b * bi + zi_b * br)):
            rows = bbar.reshape(V7X_LANES, SSM_STATE)
            tiled = jnp.concatenate([rows] * GROUPS_PER_BLOCK, axis=1)
            bw_ref[cb, :, part * STATES_PER_BLOCK:(part + 1) * STATES_PER_BLOCK] = (
                jnp.where(b_mask, tiled, 0.0).astype(BF16))
        for part, c_ref in enumerate((cr_ref, ci_ref)):
            rows = c_ref[gs].reshape(STATES_PER_BLOCK, SSM_GROUP)
            tiled = jnp.concatenate([rows] * GROUPS_PER_BLOCK, axis=1)
            signed = tiled if part == 0 else -tiled
            cw_ref[cb, part * STATES_PER_BLOCK:(part + 1) * STATES_PER_BLOCK, :] = (
                jnp.where(c_mask, signed, 0.0).astype(BF16))


def _ssmprep(a_re, a_im, log_dt, b_re_t, b_im_t, c_re_t, c_im_t):
    g, n = a_re.shape
    return pl.pallas_call(
        _ssmprep_kernel,
        out_shape=[jax.ShapeDtypeStruct((g, n), F32), jax.ShapeDtypeStruct((g, n), F32),
                   jax.ShapeDtypeStruct((CH_BLOCKS, V7X_LANES, 2 * STATES_PER_BLOCK), BF16),
                   jax.ShapeDtypeStruct((CH_BLOCKS, 2 * STATES_PER_BLOCK, V7X_LANES), BF16)],
        name="ssmprep",
    )(a_re, a_im, log_dt, b_re_t, b_im_t, c_re_t, c_im_t)


SLAB_PITCH = 2 * STATE_ROWS + 4


def _ssm_kernel(u_ref, bw_ref, cw_ref, lr_ref, li_ref, d_ref, wsrc_ref, gy_ref, wdst_ref, hcr_ref, hci_ref, *bufs,
                tc, nb):
    tiles = tc // V7X_SUBLANES
    wdst_ref[...] = wsrc_ref[...].astype(BF16)

    @pl.when(pl.program_id(0) == 0)
    def _():
        hcr_ref[...] = jnp.zeros_like(hcr_ref)
        hci_ref[...] = jnp.zeros_like(hci_ref)

    def tile_rows(ti, state_tile, imag):
        return pl.ds(ti * V7X_SUBLANES * SLAB_PITCH + 2 * state_tile + imag, V7X_SUBLANES, stride=SLAB_PITCH)

    def slab_rows(t, imag):
        return pl.ds(t * SLAB_PITCH + imag, STATE_ROWS, stride=2)

    def project_in(b):
        x_ref = bufs[2 * b]
        ub = u_ref[b].astype(BF16)
        for cb in range(CH_BLOCKS):
            res = jnp.dot(ub[:, cb * V7X_LANES:(cb + 1) * V7X_LANES], bw_ref[cb], preferred_element_type=F32)
            for k in range(TILES_PER_BLOCK):
                for ti in range(tiles):
                    rs = slice(ti * V7X_SUBLANES, (ti + 1) * V7X_SUBLANES)
                    st = cb * TILES_PER_BLOCK + k
                    x_ref[tile_rows(ti, st, 0), :] = res[rs, k * V7X_LANES:(k + 1) * V7X_LANES]
                    x_ref[tile_rows(ti, st, 1), :] = res[rs, STATES_PER_BLOCK + k * V7X_LANES:
                                                         STATES_PER_BLOCK + (k + 1) * V7X_LANES]

    def recur(b):
        x_ref, h_ref = bufs[2 * b], bufs[2 * b + 1]
        lr = lr_ref[...]
        li = li_ref[...]
        hr = hcr_ref[b]
        hi = hci_ref[b]
        for t in range(tc):
            nr = lr * hr - li * hi + x_ref[slab_rows(t, 0), :]
            ni = lr * hi + li * hr + x_ref[slab_rows(t, 1), :]
            h_ref[slab_rows(t, 0), :] = nr
            h_ref[slab_rows(t, 1), :] = ni
            hr, hi = nr, ni
        hcr_ref[b] = hr
        hci_ref[b] = hi

    def project_out(b):
        h_ref = bufs[2 * b + 1]
        u = u_ref[b]
        for cb in range(CH_BLOCKS):
            row_tiles = []
            for ti in range(tiles):
                parts = [h_ref[tile_rows(ti, cb * TILES_PER_BLOCK + k, imag), :]
                         for imag in (0, 1) for k in range(TILES_PER_BLOCK)]
                row_tiles.append(jnp.concatenate(parts, axis=-1))
            hcat = jnp.concatenate(row_tiles, axis=0).astype(BF16)
            y = jnp.dot(hcat, cw_ref[cb], preferred_element_type=F32)
            cols = slice(cb * V7X_LANES, (cb + 1) * V7X_LANES)
            y = y + d_ref[:, cols] * u[:, cols]
            gy_ref[b, :, cols] = _gelu_tanh(y).astype(BF16)

    for b in range(nb):
        project_in(b)
    for b in range(nb):
        recur(b)
        project_out(b)


def _ssm(u3, bw, cw, lam_r, lam_i, dskip, weight):
    nb, seq, _ = u3.shape
    tc = 256
    steps = seq // tc
    wrows = weight.shape[0] // steps
    assert weight.shape[0] % steps == 0 and wrows % (2 * V7X_SUBLANES) == 0
    w_spec = pl.BlockSpec((wrows, weight.shape[1]), lambda j: (j, 0))
    slab_buf = pltpu.VMEM((tc * SLAB_PITCH, V7X_LANES), F32)
    return pl.pallas_call(
        functools.partial(_ssm_kernel, tc=tc, nb=nb),
        grid=(steps,),
        in_specs=[pl.BlockSpec((nb, tc, SSM_WIDTH), lambda j: (0, j, 0)),
                  pl.BlockSpec(bw.shape, lambda j: (0, 0, 0)),
                  pl.BlockSpec(cw.shape, lambda j: (0, 0, 0)),
                  pl.BlockSpec((STATE_ROWS, V7X_LANES), lambda j: (0, 0)),
                  pl.BlockSpec((STATE_ROWS, V7X_LANES), lambda j: (0, 0)),
                  pl.BlockSpec((1, SSM_WIDTH), lambda j: (0, 0)),
                  w_spec],
        out_specs=[pl.BlockSpec((nb, tc, SSM_WIDTH), lambda j: (0, j, 0)), w_spec],
        out_shape=[jax.ShapeDtypeStruct((nb, seq, SSM_WIDTH), BF16), jax.ShapeDtypeStruct(weight.shape, BF16)],
        scratch_shapes=[pltpu.VMEM((nb, STATE_ROWS, V7X_LANES), F32),
                        pltpu.VMEM((nb, STATE_ROWS, V7X_LANES), F32)] + [slab_buf] * (2 * nb),
        compiler_params=_cparams(("arbitrary",), 60),
        name="ssm",
    )(u3, bw, cw, lam_r, lam_i, dskip, weight)


def _mix_kernel(x_ref, h_ref, mod_ref, g2n_ref, o_ref, gy_ref, wga_ref, wgs_ref, wap_ref, wla_ref, wlb_ref,
                wout_ref, wsrc_ref, x1_ref, h2_ref, wdst_ref, *, tiles_per_seq):
    n = pl.program_id(1)
    b = pl.program_id(0) // tiles_per_seq
    wdst_ref[...] = wsrc_ref[...].astype(BF16)

    def partial_out():
        h = h_ref[...]
        ga = jnp.dot(h, wga_ref[...], preferred_element_type=F32)
        gs = jnp.dot(h, wgs_ref[...], preferred_element_type=F32)
        at = jnp.dot(o_ref[...], wap_ref[...], preferred_element_type=F32)
        gy = gy_ref[...]
        la = jnp.dot(gy, wla_ref[...], preferred_element_type=F32)
        lb = jnp.dot(gy, wlb_ref[...], preferred_element_type=F32)
        ssm = la * _sigmoid(lb)
        mixed = _sigmoid(ga) * at + _sigmoid(gs) * ssm
        return jnp.dot(mixed.astype(BF16), wout_ref[...], preferred_element_type=F32)

    @pl.when(n == 0)
    def _():
        x1_ref[...] = partial_out()

    @pl.when(n > 0)
    def _():
        x1_ref[...] += partial_out()

    @pl.when(n == pl.num_programs(1) - 1)
    def _():
        g1 = _mod_row(mod_ref, 2, b)
        geff = g2n_ref[...] * (1.0 + _mod_row(mod_ref, 4, b))
        shift = _mod_row(mod_ref, 3, b)
        chunk = 2 * V7X_SUBLANES
        for r in range(x1_ref.shape[0] // chunk):
            rows = slice(r * chunk, (r + 1) * chunk)
            x1 = x_ref[rows, :] + g1 * x1_ref[rows, :]
            x1_ref[rows, :] = x1
            h2_ref[rows, :] = (_rms(x1) * geff + shift).astype(BF16)


def _mix(x2, h, mod, g2n, o, gy3, wgates, wap, wglu, wout, seq, weight):
    t = x2.shape[0]
    tm, tn = 512, 512
    tpb = seq // tm
    nn = D_MODEL // tn
    steps = (t // tm) * nn
    wrows = weight.shape[0] // steps
    assert weight.shape[0] % steps == 0 and wrows % (2 * V7X_SUBLANES) == 0
    w_spec = pl.BlockSpec((wrows, weight.shape[1]), lambda i, n: (i * nn + n, 0))
    return pl.pallas_call(
        functools.partial(_mix_kernel, tiles_per_seq=tpb),
        grid=(t // tm, nn),
        in_specs=[pl.BlockSpec((tm, D_MODEL), lambda i, n: (jnp.where(n >= 1, i, jnp.maximum(i - 1, 0)), 0)),
                  pl.BlockSpec((tm, D_MODEL), lambda i, n: (i, 0)),
                  _mod_spec(),
                  pl.BlockSpec((1, D_MODEL), lambda i, n: (0, 0)),
                  pl.BlockSpec((tm, ATTN_WIDTH), lambda i, n: (i, 0)),
                  pl.BlockSpec((None, tm, SSM_WIDTH), lambda i, n: (i // tpb, i % tpb, 0)),
                  pl.BlockSpec((D_MODEL, tn), lambda i, n: (0, n)),
                  pl.BlockSpec((D_MODEL, tn), lambda i, n: (0, nn + n)),
                  pl.BlockSpec((ATTN_WIDTH, tn), lambda i, n: (0, n)),
                  pl.BlockSpec((SSM_WIDTH, tn), lambda i, n: (0, n)),
                  pl.BlockSpec((SSM_WIDTH, tn), lambda i, n: (0, nn + n)),
                  pl.BlockSpec((tn, D_MODEL), lambda i, n: (n, 0)),
                  w_spec],
        out_specs=[pl.BlockSpec((tm, D_MODEL), lambda i, n: (i, 0)),
                   pl.BlockSpec((tm, D_MODEL), lambda i, n: (i, 0)),
                   w_spec],
        out_shape=[jax.ShapeDtypeStruct((t, D_MODEL), F32),
                   jax.ShapeDtypeStruct((t, D_MODEL), BF16),
                   jax.ShapeDtypeStruct(weight.shape, BF16)],
        compiler_params=_cparams(("arbitrary", "arbitrary"), 60),
        name="mix",
    )(x2, h, mod, g2n, o, gy3, wgates, wgates, wap, wglu, wglu, wout, weight)


def _ffn_kernel(h_ref, halo_ref, x1_ref, mod_ref, wg_ref, wv_ref, cw_ref, cb_ref, wd_ref, gf_ref, out_ref,
                act_even, act_odd, *, tm, n_tiles, tiles_per_seq, final_norm):
    f = pl.program_id(1)
    first = (pl.program_id(0) % tiles_per_seq) == 0
    act_bufs = (act_even, act_odd)

    def down_stage(act_ref):
        out_ref[...] += jnp.dot(act_ref[...], wd_ref[...], preferred_element_type=F32)

    def step(act_out, act_in):
        h = h_ref[...]
        wg = wg_ref[...]
        gate = jnp.dot(h, wg, preferred_element_type=F32)
        val = jnp.dot(h, wv_ref[...], preferred_element_type=F32)
        halo = jnp.dot(halo_ref[...], wg, preferred_element_type=F32)
        if act_in is not None:
            down_stage(act_in)
        ext = jnp.concatenate([jnp.where(first, 0.0, halo), gate], axis=0)
        g1 = pltpu.roll(ext, 1, 0)[V7X_SUBLANES:]
        g2 = pltpu.roll(ext, 2, 0)[V7X_SUBLANES:]
        conv = cw_ref[0:1, :] * g2 + cw_ref[1:2, :] * g1 + cw_ref[2:3, :] * gate + cb_ref[...]
        act_out[...] = (_silu(conv) * val).astype(BF16)

    @pl.when(f == 0)
    def _():
        out_ref[...] = jnp.zeros_like(out_ref)
        step(act_bufs[0], None)

    for parity in range(2):
        @pl.when((f > 0) & (f < n_tiles) & (f % 2 == parity))
        def _():
            step(act_bufs[parity], act_bufs[1 - parity])

    @pl.when(f == n_tiles)
    def _():
        down_stage(act_bufs[(n_tiles - 1) % 2])
        x2 = x1_ref[...] + _mod_row(mod_ref, 5, pl.program_id(0) // tiles_per_seq) * out_ref[...]
        out_ref[...] = _rms(x2) * gf_ref[...] if final_norm else x2


def _ffn(h2, x1, mod, wup, conv_w, conv_b, wdown, gf, seq, final_norm):
    t = h2.shape[0]
    tm, tf = 1024, 512
    tpb = seq // tm
    nf = D_FF // tf
    hb = tm // V7X_SUBLANES
    return pl.pallas_call(
        functools.partial(_ffn_kernel, tm=tm, n_tiles=nf, tiles_per_seq=tpb, final_norm=final_norm),
        grid=(t // tm, nf + 1),
        in_specs=[pl.BlockSpec((tm, D_MODEL), lambda i, f: (i, 0)),
                  pl.BlockSpec((V7X_SUBLANES, D_MODEL), lambda i, f: (jnp.maximum(i * hb - 1, 0), 0)),
                  pl.BlockSpec((tm, D_MODEL), lambda i, f: (jnp.where(f >= 4, i, jnp.maximum(i - 1, 0)), 0)),
                  _mod_spec(),
                  pl.BlockSpec((D_MODEL, tf), lambda i, f: (0, jnp.minimum(f, nf - 1))),
                  pl.BlockSpec((D_MODEL, tf), lambda i, f: (0, nf + jnp.minimum(f, nf - 1))),
                  pl.BlockSpec((3, tf), lambda i, f: (0, jnp.minimum(f, nf - 1))),
                  pl.BlockSpec((1, tf), lambda i, f: (0, jnp.minimum(f, nf - 1))),
                  pl.BlockSpec((tf, D_MODEL), lambda i, f: (jnp.maximum(f - 1, 0), 0)),
                  pl.BlockSpec((1, D_MODEL), lambda i, f: (0, 0))],
        out_specs=pl.BlockSpec((tm, D_MODEL), lambda i, f: (i, 0)),
        out_shape=jax.ShapeDtypeStruct((t, D_MODEL), F32),
        scratch_shapes=[pltpu.VMEM((tm, tf), BF16), pltpu.VMEM((tm, tf), BF16)],
        compiler_params=_cparams(("arbitrary", "arbitrary"), 63),
        name="ffn",
    )(h2, h2, x1, mod, wup, wup, conv_w, conv_b, wdown, gf)


def kernel(x, c, ada_w, ada_b, norm_mix_g, w_in, attn_sinks, w_attn_proj, ssm_a_re, ssm_a_im, ssm_log_dt,
           ssm_b_re, ssm_b_im, ssm_c_re, ssm_c_im, ssm_d, w_ssm_glu, w_out, norm_ffn_g, w_ffn_up, ffn_conv_w,
           ffn_conv_b, w_ffn_down, final_g):
    bsz, seq, _ = x.shape
    x2 = x.reshape(bsz * seq, D_MODEL)
    for i in range(ada_w.shape[0]):
        mod = _ada(c, ada_w[i], ada_b[i].reshape(1, -1))

        win = w_in[i]
        g1n = norm_mix_g[i].reshape(1, D_MODEL)
        g2n = norm_ffn_g[i].reshape(1, D_MODEL)

        h, qkv, u, w_gates = _inproj(x2, mod, g1n, win, seq)
        o, w_ap, w_glu, w_o = _attention(attn_sinks[i], qkv, seq, (w_attn_proj[i], w_ssm_glu[i], w_out[i]))

        lam_r, lam_i, bw, cw = _ssmprep(
            ssm_a_re[i], ssm_a_im[i], ssm_log_dt[i].reshape(-1, 1),
            ssm_b_re[i].transpose(0, 2, 1), ssm_b_im[i].transpose(0, 2, 1),
            ssm_c_re[i].transpose(0, 2, 1), ssm_c_im[i].transpose(0, 2, 1))
        gy, w_down = _ssm(u.reshape(bsz, seq, SSM_WIDTH), bw, cw,
                          lam_r.reshape(STATE_ROWS, V7X_LANES), lam_i.reshape(STATE_ROWS, V7X_LANES),
                          ssm_d[i].reshape(1, SSM_WIDTH), w_ffn_down[i])

        x2, h2, w_up = _mix(x2, h, mod, g2n, o, gy, w_gates, w_ap, w_glu, w_o, seq, w_ffn_up[i])
        x2 = _ffn(h2, x2, mod, w_up, ffn_conv_w[i], ffn_conv_b[i].reshape(1, D_FF),
                  w_down, final_g.reshape(1, D_MODEL), seq, i == ada_w.shape[0] - 1)
    return x2.reshape(bsz, seq, D_MODEL)
```

```python
import functools

import jax
import jax.numpy as jnp
import numpy as np
from jax import lax
from jax.experimental import pallas as pl
from jax.experimental.pallas import tpu as pltpu

F32 = jnp.float32
BF16 = jnp.bfloat16

D_MODEL = 2048
HEAD_DIM = 64
N_Q_HEADS = 16
N_KV_HEADS = 2
Q_PER_KV = N_Q_HEADS // N_KV_HEADS
ATTN_WIDTH = N_Q_HEADS * HEAD_DIM
KV_WIDTH = N_KV_HEADS * HEAD_DIM
QKV_WIDTH = ATTN_WIDTH + 2 * KV_WIDTH
WINDOW = 128
BLOCK = 128
NEG_INF = -1e30
SSM_WIDTH = D_MODEL // 2
SSM_GROUP = 16
N_SSM_GROUPS = SSM_WIDTH // SSM_GROUP
SSM_STATE = 64
N_STATES = N_SSM_GROUPS * SSM_STATE
D_FF = 5632
RMS_EPS = 1e-6

V7X_SUBLANES = 8
V7X_LANES = 128
STATE_ROWS = N_STATES // V7X_LANES
CH_BLOCKS = SSM_WIDTH // V7X_LANES
GROUPS_PER_BLOCK = V7X_LANES // SSM_GROUP
STATES_PER_BLOCK = GROUPS_PER_BLOCK * SSM_STATE
TILES_PER_BLOCK = STATES_PER_BLOCK // V7X_LANES


def _cparams(sem, vmem_mb):
    return pltpu.CompilerParams(dimension_semantics=sem, vmem_limit_bytes=vmem_mb * 1024 * 1024)


def _sigmoid(x):
    return 1.0 / (1.0 + jnp.exp(-x))


def _silu(x):
    return x * _sigmoid(x)


def _gelu_tanh(x):
    c = np.float32(np.sqrt(2.0 / np.pi))
    return x * (0.5 * (1.0 + jnp.tanh(c * (x + np.float32(0.044715) * (x * x * x)))))


def _rms(x):
    return x * lax.rsqrt(jnp.mean(x * x, axis=-1, keepdims=True) + RMS_EPS)


def _modulated_norm(x, g, scale, shift):
    return _rms(x) * (g * (1.0 + scale)) + shift


N_MOD = 6
MOD_ROWS = V7X_SUBLANES


def _ada_kernel(c_ref, w_ref, b_ref, o_ref):
    c = c_ref[...]
    cond = jnp.concatenate([_silu(c), jnp.zeros((MOD_ROWS - c.shape[0], c.shape[1]), F32)], axis=0)
    o_ref[...] = jnp.dot(cond.astype(BF16), w_ref[...].astype(BF16), preferred_element_type=F32) + b_ref[...]


def _ada(c, w, b):
    assert c.shape[0] <= MOD_ROWS and w.shape[1] == N_MOD * D_MODEL
    tn = 1024
    per = D_MODEL // tn
    return pl.pallas_call(
        _ada_kernel,
        grid=(N_MOD * per,),
        in_specs=[pl.BlockSpec(c.shape, lambda j: (0, 0)),
                  pl.BlockSpec((D_MODEL, tn), lambda j: (0, j)),
                  pl.BlockSpec((1, tn), lambda j: (0, j))],
        out_specs=pl.BlockSpec((None, MOD_ROWS, tn), lambda j: (j // per, 0, j % per)),
        out_shape=jax.ShapeDtypeStruct((N_MOD, MOD_ROWS, D_MODEL), F32),
        compiler_params=_cparams(("arbitrary",), 40),
        name="ada",
    )(c, w, b)


def _mod_spec():
    return pl.BlockSpec((N_MOD, MOD_ROWS, D_MODEL), lambda *_: (0, 0, 0))


def _mod_row(mod_ref, k, b):
    return mod_ref[k, pl.ds(b, 1), :]


def _inproj_kernel(x_ref, mod_ref, g_ref, w_ref, wgate_ref, h_ref, qkv_ref, u_ref, wgate_bf_ref, wb_scr,
                   *, tiles_per_seq):
    @pl.when(pl.program_id(0) == 0)
    def _():
        wb_scr[...] = w_ref[...].astype(BF16)

    wgate_bf_ref[...] = wgate_ref[...].astype(BF16)

    b = pl.program_id(0) // tiles_per_seq
    h = _modulated_norm(x_ref[...], g_ref[...], _mod_row(mod_ref, 1, b), _mod_row(mod_ref, 0, b)).astype(BF16)
    h_ref[...] = h
    p = jnp.dot(h, wb_scr[...], preferred_element_type=F32)
    qkv_ref[...] = p[:, :QKV_WIDTH].astype(BF16)
    u_ref[...] = p[:, QKV_WIDTH:]


MIX_TN = 512


def _tiled_cols_spec(rows, cols, steps):
    cw = cols // steps
    assert cols % steps == 0 and cw % V7X_LANES == 0 and MIX_TN % cw == 0 and cols % MIX_TN == 0
    per = MIX_TN // cw
    spec = pl.BlockSpec((None, rows, cw), lambda i: (i // per, 0, i % per))
    return spec, jax.ShapeDtypeStruct((cols // MIX_TN, rows, MIX_TN), BF16)


def _inproj(x2, mod, g, w_in, seq):
    t = x2.shape[0]
    tm = 512
    tpb = seq // tm
    n = QKV_WIDTH + SSM_WIDTH
    steps = t // tm
    gate_cols = w_in.shape[1] - n
    gw = gate_cols // steps
    assert gate_cols % steps == 0 and n % gw == 0 and gw % V7X_LANES == 0
    gate_spec, gate_shape = _tiled_cols_spec(D_MODEL, gate_cols, steps)
    return pl.pallas_call(
        functools.partial(_inproj_kernel, tiles_per_seq=tpb),
        grid=(steps,),
        in_specs=[pl.BlockSpec((tm, D_MODEL), lambda i: (i, 0)),
                  _mod_spec(),
                  pl.BlockSpec((1, D_MODEL), lambda i: (0, 0)),
                  pl.BlockSpec((D_MODEL, n), lambda i: (0, 0), pipeline_mode=pl.Buffered(1)),
                  pl.BlockSpec((D_MODEL, gw), lambda i: (0, n // gw + i))],
        out_specs=[pl.BlockSpec((tm, D_MODEL), lambda i: (i, 0)),
                   pl.BlockSpec((tm, QKV_WIDTH), lambda i: (i, 0)),
                   pl.BlockSpec((tm, SSM_WIDTH), lambda i: (i, 0)),
                   gate_spec],
        out_shape=[jax.ShapeDtypeStruct((t, D_MODEL), BF16),
                   jax.ShapeDtypeStruct((t, QKV_WIDTH), BF16),
                   jax.ShapeDtypeStruct((t, SSM_WIDTH), F32),
                   gate_shape],
        scratch_shapes=[pltpu.VMEM((D_MODEL, n), BF16)],
        compiler_params=_cparams(("arbitrary",), 60),
        name="inproj",
    )(x2, mod, g, w_in, w_in)


def _attn_kernel(sink_ref, q_ref, kvc_ref, kvp_ref, w0_ref, w1_ref, w2_ref, o_ref, w0b_ref, w1b_ref, w2b_ref,
                 s_scr, p_scr, t_scr, *, tq, tiles_per_seq):
    for src, dst in ((w0_ref, w0b_ref), (w1_ref, w1b_ref), (w2_ref, w2b_ref)):
        dst[...] = src[...].astype(BF16)

    first = (pl.program_id(0) % tiles_per_seq) == 0
    qi = lax.broadcasted_iota(jnp.int32, (BLOCK, BLOCK), 0)
    kj = lax.broadcasted_iota(jnp.int32, (BLOCK, BLOCK), 1)
    upper = kj > qi
    pairs = Q_PER_KV // 2
    pair_w = 2 * HEAD_DIM
    zeros = jnp.zeros((2 * BLOCK, HEAD_DIM), BF16)
    ones = jnp.ones((2 * BLOCK, pair_w), BF16)
    scale = jnp.asarray(HEAD_DIM ** -0.5, BF16)
    contract_lanes = (((1,), (1,)), ((), ()))
    n_qb = tq // BLOCK

    def window(qb):
        r0 = qb * BLOCK
        prev = kvp_ref[...] if qb == 0 else kvc_ref[r0 - BLOCK:r0, :]
        return jnp.concatenate([prev, kvc_ref[r0:r0 + BLOCK, :]], axis=0)

    def slot(qb, g, par):
        return (qb * N_KV_HEADS + g) * 2 + par

    for qb in range(n_qb):
        r0 = qb * BLOCK
        win = window(qb)
        for g in range(N_KV_HEADS):
            kg = win[:, g * HEAD_DIM:(g + 1) * HEAD_DIM]
            kz = (jnp.concatenate([kg, zeros], axis=1), jnp.concatenate([zeros, kg], axis=1))
            c0 = g * Q_PER_KV * HEAD_DIM
            qg = q_ref[r0:r0 + BLOCK, c0:c0 + Q_PER_KV * HEAD_DIM] * scale
            qs = jnp.concatenate([qg[:, p * pair_w:(p + 1) * pair_w] for p in range(pairs)], axis=0)
            for par in range(2):
                s_scr[slot(qb, g, par)] = lax.dot_general(qs, kz[par], contract_lanes, preferred_element_type=F32)

    for qb in range(n_qb):
        pad_bias = jnp.where(first, NEG_INF, 0.0).astype(F32) if qb == 0 else None
        for g in range(N_KV_HEADS):
            for par in range(2):
                sl = slot(qb, g, par)
                for p in range(pairs):
                    rows = slice(p * BLOCK, (p + 1) * BLOCK)
                    s_prev = s_scr[sl, rows, :BLOCK]
                    if pad_bias is not None:
                        s_prev = s_prev + pad_bias
                    sm = jnp.where(upper, s_prev, s_scr[sl, rows, BLOCK:])
                    sink = sink_ref[g * Q_PER_KV + 2 * p + par]
                    m = jnp.maximum(jnp.max(sm, axis=-1, keepdims=True), sink)
                    m = jnp.broadcast_to(m, (BLOCK, BLOCK))
                    pe = jnp.exp(sm - m)
                    p_scr[sl, rows, :BLOCK] = jnp.where(upper, pe, 0.0).astype(BF16)
                    p_scr[sl, rows, BLOCK:] = jnp.where(upper, 0.0, pe).astype(BF16)
                    t_scr[sl, rows, :] = jnp.exp(sink - m)

    for qb in range(n_qb):
        r0 = qb * BLOCK
        win = window(qb)
        for g in range(N_KV_HEADS):
            vg = win[:, KV_WIDTH + g * HEAD_DIM:KV_WIDTH + (g + 1) * HEAD_DIM]
            vz = (jnp.concatenate([vg, zeros, ones], axis=1), jnp.concatenate([zeros, vg, ones], axis=1))
            c0 = g * Q_PER_KV * HEAD_DIM
            acc = None
            for par in range(2):
                sl = slot(qb, g, par)
                res = jnp.dot(p_scr[sl], vz[par], preferred_element_type=F32)
                on = res[:, :pair_w] / (res[:, pair_w:] + t_scr[sl])
                acc = on if acc is None else acc + on
            for p in range(pairs):
                o_ref[r0:r0 + BLOCK, c0 + p * pair_w:c0 + (p + 1) * pair_w] = (
                    acc[p * BLOCK:(p + 1) * BLOCK].astype(BF16))


def _attention(sinks, qkv, seq, weights, tile_major):
    t = qkv.shape[0]
    tq = 512
    tiles_per_seq = seq // tq
    kv_col = ATTN_WIDTH // (2 * KV_WIDTH)
    rb = tq // BLOCK
    steps = t // tq
    slots = rb * N_KV_HEADS * 2
    rows = (Q_PER_KV // 2) * BLOCK
    for w in weights:
        assert w.shape[1] % (steps * V7X_LANES) == 0
    w_specs = [pl.BlockSpec((w.shape[0], w.shape[1] // steps), lambda i: (0, i)) for w in weights]
    wo_specs, wo_shapes = [], []
    for w, spec, tiled in zip(weights, w_specs, tile_major):
        if tiled:
            spec, shape = _tiled_cols_spec(w.shape[0], w.shape[1], steps)
        else:
            shape = jax.ShapeDtypeStruct(w.shape, BF16)
        wo_specs.append(spec)
        wo_shapes.append(shape)
    return pl.pallas_call(
        functools.partial(_attn_kernel, tq=tq, tiles_per_seq=tiles_per_seq),
        grid=(steps,),
        in_specs=[pl.BlockSpec(memory_space=pltpu.SMEM),
                  pl.BlockSpec((tq, ATTN_WIDTH), lambda i: (i, 0)),
                  pl.BlockSpec((tq, 2 * KV_WIDTH), lambda i: (i, kv_col)),
                  pl.BlockSpec((BLOCK, 2 * KV_WIDTH), lambda i: (jnp.maximum(i * rb - 1, 0), kv_col))] + w_specs,
        out_specs=[pl.BlockSpec((tq, ATTN_WIDTH), lambda i: (i, 0))] + wo_specs,
        out_shape=[jax.ShapeDtypeStruct((t, ATTN_WIDTH), BF16)] + wo_shapes,
        scratch_shapes=[pltpu.VMEM((slots, rows, 2 * BLOCK), F32),
                        pltpu.VMEM((slots, rows, 2 * BLOCK), BF16),
                        pltpu.VMEM((slots, rows, BLOCK), F32)],
        compiler_params=_cparams(("arbitrary",), 48),
        name="attn",
    )(sinks, qkv, qkv, qkv, *weights)


def _ssmprep_kernel(ar_ref, ai_ref, ldt_ref, br_ref, bi_ref, cr_ref, ci_ref, lr_ref, li_ref, bw_ref, cw_ref):
    ar = ar_ref[...]
    ai = ai_ref[...]
    dt = jnp.exp(ldt_ref[...])
    mag = jnp.exp(ar * dt)
    lr = mag * jnp.cos(ai * dt)
    li = mag * jnp.sin(ai * dt)
    den = ar * ar + ai * ai
    zr = ((lr - 1.0) * ar + li * ai) / den
    zi = (li * ar - (lr - 1.0) * ai) / den
    lr_ref[...] = lr
    li_ref[...] = li

    def same_group(shape, row_w, col_w):
        rows = lax.broadcasted_iota(jnp.int32, shape, 0) // row_w
        cols = lax.broadcasted_iota(jnp.int32, shape, 1) // col_w
        return rows == cols

    b_mask = same_group((V7X_LANES, STATES_PER_BLOCK), SSM_GROUP, SSM_STATE)
    c_mask = same_group((STATES_PER_BLOCK, V7X_LANES), SSM_STATE, SSM_GROUP)
    for cb in range(CH_BLOCKS):
        gs = slice(cb * GROUPS_PER_BLOCK, (cb + 1) * GROUPS_PER_BLOCK)
        zr_b = zr[gs][:, None, :]
        zi_b = zi[gs][:, None, :]
        br = br_ref[gs]
        bi = bi_ref[gs]
        for part, bbar in enumerate((zr_b * br - zi_b * bi, zr_b * bi + zi_b * br)):
            rows = bbar.reshape(V7X_LANES, SSM_STATE)
            tiled = jnp.concatenate([rows] * GROUPS_PER_BLOCK, axis=1)
            bw_ref[cb, :, part * STATES_PER_BLOCK:(part + 1) * STATES_PER_BLOCK] = (
                jnp.where(b_mask, tiled, 0.0).astype(BF16))
        for part, c_ref in enumerate((cr_ref, ci_ref)):
            rows = c_ref[gs].reshape(STATES_PER_BLOCK, SSM_GROUP)
            tiled = jnp.concatenate([rows] * GROUPS_PER_BLOCK, axis=1)
            signed = tiled if part == 0 else -tiled
            cw_ref[cb, part * STATES_PER_BLOCK:(part + 1) * STATES_PER_BLOCK, :] = (
                jnp.where(c_mask, signed, 0.0).astype(BF16))


def _ssmprep(a_re, a_im, log_dt, b_re_t, b_im_t, c_re_t, c_im_t):
    g, n = a_re.shape
    return pl.pallas_call(
        _ssmprep_kernel,
        out_shape=[jax.ShapeDtypeStruct((g, n), F32), jax.ShapeDtypeStruct((g, n), F32),
                   jax.ShapeDtypeStruct((CH_BLOCKS, V7X_LANES, 2 * STATES_PER_BLOCK), BF16),
                   jax.ShapeDtypeStruct((CH_BLOCKS, 2 * STATES_PER_BLOCK, V7X_LANES), BF16)],
        name="ssmprep",
    )(a_re, a_im, log_dt, b_re_t, b_im_t, c_re_t, c_im_t)


SLAB_PITCH = 2 * STATE_ROWS + 4


def _ssm_kernel(u_ref, bw_ref, cw_ref, lr_ref, li_ref, d_ref, wsrc_ref, gy_ref, wdst_ref, hcr_ref, hci_ref, *bufs,
                tc, nb):
    tiles = tc // V7X_SUBLANES
    wdst_ref[...] = wsrc_ref[...].astype(BF16)

    @pl.when(pl.program_id(0) == 0)
    def _():
        hcr_ref[...] = jnp.zeros_like(hcr_ref)
        hci_ref[...] = jnp.zeros_like(hci_ref)

    def tile_rows(ti, state_tile, imag):
        return pl.ds(ti * V7X_SUBLANES * SLAB_PITCH + 2 * state_tile + imag, V7X_SUBLANES, stride=SLAB_PITCH)

    def slab_rows(t, imag):
        return pl.ds(t * SLAB_PITCH + imag, STATE_ROWS, stride=2)

    def project_in(b):
        x_ref = bufs[2 * b]
        ub = u_ref[b].astype(BF16)
        for cb in range(CH_BLOCKS):
            res = jnp.dot(ub[:, cb * V7X_LANES:(cb + 1) * V7X_LANES], bw_ref[cb], preferred_element_type=F32)
            for k in range(TILES_PER_BLOCK):
                for ti in range(tiles):
                    rs = slice(ti * V7X_SUBLANES, (ti + 1) * V7X_SUBLANES)
                    st = cb * TILES_PER_BLOCK + k
                    x_ref[tile_rows(ti, st, 0), :] = res[rs, k * V7X_LANES:(k + 1) * V7X_LANES]
                    x_ref[tile_rows(ti, st, 1), :] = res[rs, STATES_PER_BLOCK + k * V7X_LANES:
                                                         STATES_PER_BLOCK + (k + 1) * V7X_LANES]

    def recur(b):
        x_ref, h_ref = bufs[2 * b], bufs[2 * b + 1]
        lr = lr_ref[...]
        li = li_ref[...]
        hr = hcr_ref[b]
        hi = hci_ref[b]
        for t in range(tc):
            nr = lr * hr - li * hi + x_ref[slab_rows(t, 0), :]
            ni = lr * hi + li * hr + x_ref[slab_rows(t, 1), :]
            h_ref[slab_rows(t, 0), :] = nr
            h_ref[slab_rows(t, 1), :] = ni
            hr, hi = nr, ni
        hcr_ref[b] = hr
        hci_ref[b] = hi

    def project_out(b):
        h_ref = bufs[2 * b + 1]
        u = u_ref[b]
        for cb in range(CH_BLOCKS):
            row_tiles = []
            for ti in range(tiles):
                parts = [h_ref[tile_rows(ti, cb * TILES_PER_BLOCK + k, imag), :]
                         for imag in (0, 1) for k in range(TILES_PER_BLOCK)]
                row_tiles.append(jnp.concatenate(parts, axis=-1))
            hcat = jnp.concatenate(row_tiles, axis=0).astype(BF16)
            y = jnp.dot(hcat, cw_ref[cb], preferred_element_type=F32)
            cols = slice(cb * V7X_LANES, (cb + 1) * V7X_LANES)
            y = y + d_ref[:, cols] * u[:, cols]
            gy_ref[b, :, cols] = _gelu_tanh(y).astype(BF16)

    for b in range(nb):
        project_in(b)
    for b in range(nb):
        recur(b)
        project_out(b)


def _ssm(u3, bw, cw, lam_r, lam_i, dskip, weight):
    nb, seq, _ = u3.shape
    tc = 256
    steps = seq // tc
    wrows = weight.shape[0] // steps
    assert weight.shape[0] % steps == 0 and wrows % (2 * V7X_SUBLANES) == 0
    w_spec = pl.BlockSpec((wrows, weight.shape[1]), lambda j: (j, 0))
    slab_buf = pltpu.VMEM((tc * SLAB_PITCH, V7X_LANES), F32)
    return pl.pallas_call(
        functools.partial(_ssm_kernel, tc=tc, nb=nb),
        grid=(steps,),
        in_specs=[pl.BlockSpec((nb, tc, SSM_WIDTH), lambda j: (0, j, 0)),
                  pl.BlockSpec(bw.shape, lambda j: (0, 0, 0)),
                  pl.BlockSpec(cw.shape, lambda j: (0, 0, 0)),
                  pl.BlockSpec((STATE_ROWS, V7X_LANES), lambda j: (0, 0)),
                  pl.BlockSpec((STATE_ROWS, V7X_LANES), lambda j: (0, 0)),
                  pl.BlockSpec((1, SSM_WIDTH), lambda j: (0, 0)),
                  w_spec],
        out_specs=[pl.BlockSpec((nb, tc, SSM_WIDTH), lambda j: (0, j, 0)), w_spec],
        out_shape=[jax.ShapeDtypeStruct((nb, seq, SSM_WIDTH), BF16), jax.ShapeDtypeStruct(weight.shape, BF16)],
        scratch_shapes=[pltpu.VMEM((nb, STATE_ROWS, V7X_LANES), F32),
                        pltpu.VMEM((nb, STATE_ROWS, V7X_LANES), F32)] + [slab_buf] * (2 * nb),
        compiler_params=_cparams(("arbitrary",), 60),
        name="ssm",
    )(u3, bw, cw, lam_r, lam_i, dskip, weight)


def _mix_kernel(x_ref, h_ref, mod_ref, g2n_ref, o_ref, gy_ref, wga_ref, wgs_ref, wap_ref, wla_ref, wlb_ref,
                wout_ref, wsrc_ref, x1_ref, h2_ref, wdst_ref, *, tiles_per_seq):
    n = pl.program_id(1)
    b = pl.program_id(0) // tiles_per_seq
    wdst_ref[...] = wsrc_ref[...].astype(BF16)

    def partial_out():
        h = h_ref[...]
        ga = jnp.dot(h, wga_ref[...], preferred_element_type=F32)
        gs = jnp.dot(h, wgs_ref[...], preferred_element_type=F32)
        at = jnp.dot(o_ref[...], wap_ref[...], preferred_element_type=F32)
        gy = gy_ref[...]
        la = jnp.dot(gy, wla_ref[...], preferred_element_type=F32)
        lb = jnp.dot(gy, wlb_ref[...], preferred_element_type=F32)
        ssm = la * _sigmoid(lb)
        mixed = _sigmoid(ga) * at + _sigmoid(gs) * ssm
        return jnp.dot(mixed.astype(BF16), wout_ref[...], preferred_element_type=F32)

    @pl.when(n == 0)
    def _():
        x1_ref[...] = partial_out()

    @pl.when(n > 0)
    def _():
        x1_ref[...] += partial_out()

    @pl.when(n == pl.num_programs(1) - 1)
    def _():
        g1 = _mod_row(mod_ref, 2, b)
        geff = g2n_ref[...] * (1.0 + _mod_row(mod_ref, 4, b))
        shift = _mod_row(mod_ref, 3, b)
        chunk = 2 * V7X_SUBLANES
        for r in range(x1_ref.shape[0] // chunk):
            rows = slice(r * chunk, (r + 1) * chunk)
            x1 = x_ref[rows, :] + g1 * x1_ref[rows, :]
            x1_ref[rows, :] = x1
            h2_ref[rows, :] = (_rms(x1) * geff + shift).astype(BF16)


def _mix(x2, h, mod, g2n, o, gy3, wgates, wap, wglu, wout, seq, weight):
    t = x2.shape[0]
    tm, tn = 512, MIX_TN
    tpb = seq // tm
    nn = D_MODEL // tn
    steps = (t // tm) * nn
    wrows = weight.shape[0] // steps
    assert weight.shape[0] % steps == 0 and wrows % (2 * V7X_SUBLANES) == 0
    w_spec = pl.BlockSpec((wrows, weight.shape[1]), lambda i, n: (i * nn + n, 0))
    return pl.pallas_call(
        functools.partial(_mix_kernel, tiles_per_seq=tpb),
        grid=(t // tm, nn),
        in_specs=[pl.BlockSpec((tm, D_MODEL), lambda i, n: (jnp.where(n >= 1, i, jnp.maximum(i - 1, 0)), 0)),
                  pl.BlockSpec((tm, D_MODEL), lambda i, n: (i, 0)),
                  _mod_spec(),
                  pl.BlockSpec((1, D_MODEL), lambda i, n: (0, 0)),
                  pl.BlockSpec((tm, ATTN_WIDTH), lambda i, n: (i, 0)),
                  pl.BlockSpec((None, tm, SSM_WIDTH), lambda i, n: (i // tpb, i % tpb, 0)),
                  pl.BlockSpec((None, D_MODEL, tn), lambda i, n: (n, 0, 0)),
                  pl.BlockSpec((None, D_MODEL, tn), lambda i, n: (nn + n, 0, 0)),
                  pl.BlockSpec((None, ATTN_WIDTH, tn), lambda i, n: (n, 0, 0)),
                  pl.BlockSpec((None, SSM_WIDTH, tn), lambda i, n: (n, 0, 0)),
                  pl.BlockSpec((None, SSM_WIDTH, tn), lambda i, n: (nn + n, 0, 0)),
                  pl.BlockSpec((tn, D_MODEL), lambda i, n: (n, 0)),
                  w_spec],
        out_specs=[pl.BlockSpec((tm, D_MODEL), lambda i, n: (i, 0)),
                   pl.BlockSpec((tm, D_MODEL), lambda i, n: (i, 0)),
                   w_spec],
        out_shape=[jax.ShapeDtypeStruct((t, D_MODEL), F32),
                   jax.ShapeDtypeStruct((t, D_MODEL), BF16),
                   jax.ShapeDtypeStruct(weight.shape, BF16)],
        compiler_params=_cparams(("arbitrary", "arbitrary"), 60),
        name="mix",
    )(x2, h, mod, g2n, o, gy3, wgates, wgates, wap, wglu, wglu, wout, weight)


def _ffn_kernel(h_ref, halo_ref, x1_ref, mod_ref, wg_ref, wv_ref, cw_ref, cb_ref, wd_ref, gf_ref, out_ref,
                act_even, act_odd, *, tm, n_tiles, tiles_per_seq, final_norm):
    f = pl.program_id(1)
    first = (pl.program_id(0) % tiles_per_seq) == 0
    act_bufs = (act_even, act_odd)

    def down_stage(act_ref):
        out_ref[...] += jnp.dot(act_ref[...], wd_ref[...], preferred_element_type=F32)

    def step(act_out, act_in):
        h = h_ref[...]
        wg = wg_ref[...]
        gate = jnp.dot(h, wg, preferred_element_type=F32)
        val = jnp.dot(h, wv_ref[...], preferred_element_type=F32)
        halo = jnp.dot(halo_ref[...], wg, preferred_element_type=F32)
        if act_in is not None:
            down_stage(act_in)
        ext = jnp.concatenate([jnp.where(first, 0.0, halo), gate], axis=0)
        g1 = pltpu.roll(ext, 1, 0)[V7X_SUBLANES:]
        g2 = pltpu.roll(ext, 2, 0)[V7X_SUBLANES:]
        conv = cw_ref[0:1, :] * g2 + cw_ref[1:2, :] * g1 + cw_ref[2:3, :] * gate + cb_ref[...]
        act_out[...] = (_silu(conv) * val).astype(BF16)

    @pl.when(f == 0)
    def _():
        out_ref[...] = jnp.zeros_like(out_ref)
        step(act_bufs[0], None)

    for parity in range(2):
        @pl.when((f > 0) & (f < n_tiles) & (f % 2 == parity))
        def _():
            step(act_bufs[parity], act_bufs[1 - parity])

    @pl.when(f == n_tiles)
    def _():
        down_stage(act_bufs[(n_tiles - 1) % 2])
        x2 = x1_ref[...] + _mod_row(mod_ref, 5, pl.program_id(0) // tiles_per_seq) * out_ref[...]
        out_ref[...] = _rms(x2) * gf_ref[...] if final_norm else x2


def _ffn(h2, x1, mod, wup, conv_w, conv_b, wdown, gf, seq, final_norm):
    t = h2.shape[0]
    tm, tf = 1024, 512
    tpb = seq // tm
    nf = D_FF // tf
    hb = tm // V7X_SUBLANES
    return pl.pallas_call(
        functools.partial(_ffn_kernel, tm=tm, n_tiles=nf, tiles_per_seq=tpb, final_norm=final_norm),
        grid=(t // tm, nf + 1),
        in_specs=[pl.BlockSpec((tm, D_MODEL), lambda i, f: (i, 0)),
                  pl.BlockSpec((V7X_SUBLANES, D_MODEL), lambda i, f: (jnp.maximum(i * hb - 1, 0), 0)),
                  pl.BlockSpec((tm, D_MODEL), lambda i, f: (jnp.where(f >= 4, i, jnp.maximum(i - 1, 0)), 0)),
                  _mod_spec(),
                  pl.BlockSpec((D_MODEL, tf), lambda i, f: (0, jnp.minimum(f, nf - 1))),
                  pl.BlockSpec((D_MODEL, tf), lambda i, f: (0, nf + jnp.minimum(f, nf - 1))),
                  pl.BlockSpec((3, tf), lambda i, f: (0, jnp.minimum(f, nf - 1))),
                  pl.BlockSpec((1, tf), lambda i, f: (0, jnp.minimum(f, nf - 1))),
                  pl.BlockSpec((tf, D_MODEL), lambda i, f: (jnp.maximum(f - 1, 0), 0)),
                  pl.BlockSpec((1, D_MODEL), lambda i, f: (0, 0))],
        out_specs=pl.BlockSpec((tm, D_MODEL), lambda i, f: (i, 0)),
        out_shape=jax.ShapeDtypeStruct((t, D_MODEL), F32),
        scratch_shapes=[pltpu.VMEM((tm, tf), BF16), pltpu.VMEM((tm, tf), BF16)],
        compiler_params=_cparams(("arbitrary", "arbitrary"), 63),
        name="ffn",
    )(h2, h2, x1, mod, wup, wup, conv_w, conv_b, wdown, gf)


def kernel(x, c, ada_w, ada_b, norm_mix_g, w_in, attn_sinks, w_attn_proj, ssm_a_re, ssm_a_im, ssm_log_dt,
           ssm_b_re, ssm_b_im, ssm_c_re, ssm_c_im, ssm_d, w_ssm_glu, w_out, norm_ffn_g, w_ffn_up, ffn_conv_w,
           ffn_conv_b, w_ffn_down, final_g):
    bsz, seq, _ = x.shape
    x2 = x.reshape(bsz * seq, D_MODEL)
    for i in range(ada_w.shape[0]):
        mod = _ada(c, ada_w[i], ada_b[i].reshape(1, -1))

        win = w_in[i]
        g1n = norm_mix_g[i].reshape(1, D_MODEL)
        g2n = norm_ffn_g[i].reshape(1, D_MODEL)

        h, qkv, u, w_gates = _inproj(x2, mod, g1n, win, seq)
        o, w_ap, w_glu, w_o = _attention(attn_sinks[i], qkv, seq, (w_attn_proj[i], w_ssm_glu[i], w_out[i]),
                                         tile_major=(True, True, False))

        lam_r, lam_i, bw, cw = _ssmprep(
            ssm_a_re[i], ssm_a_im[i], ssm_log_dt[i].reshape(-1, 1),
            ssm_b_re[i].transpose(0, 2, 1), ssm_b_im[i].transpose(0, 2, 1),
            ssm_c_re[i].transpose(0, 2, 1), ssm_c_im[i].transpose(0, 2, 1))
        gy, w_down = _ssm(u.reshape(bsz, seq, SSM_WIDTH), bw, cw,
                          lam_r.reshape(STATE_ROWS, V7X_LANES), lam_i.reshape(STATE_ROWS, V7X_LANES),
                          ssm_d[i].reshape(1, SSM_WIDTH), w_ffn_down[i])

        x2, h2, w_up = _mix(x2, h, mod, g2n, o, gy, w_gates, w_ap, w_glu, w_o, seq, w_ffn_up[i])
        x2 = _ffn(h2, x2, mod, w_up, ffn_conv_w[i], ffn_conv_b[i].reshape(1, D_FF),
                  w_down, final_g.reshape(1, D_MODEL), seq, i == ada_w.shape[0] - 1)
    return x2.reshape(bsz, seq, D_MODEL)
```

```python
import functools

import jax
import jax.numpy as jnp
import numpy as np
from jax import lax
from jax.experimental import pallas as pl
from jax.experimental.pallas import tpu as pltpu

F32 = jnp.float32
BF16 = jnp.bfloat16

D_MODEL = 2048
HEAD_DIM = 64
N_Q_HEADS = 16
N_KV_HEADS = 2
Q_PER_KV = N_Q_HEADS // N_KV_HEADS
ATTN_WIDTH = N_Q_HEADS * HEAD_DIM
KV_WIDTH = N_KV_HEADS * HEAD_DIM
QKV_WIDTH = ATTN_WIDTH + 2 * KV_WIDTH
WINDOW = 128
BLOCK = 128
NEG_INF = -1e30
SSM_WIDTH = D_MODEL // 2
SSM_GROUP = 16
N_SSM_GROUPS = SSM_WIDTH // SSM_GROUP
SSM_STATE = 64
N_STATES = N_SSM_GROUPS * SSM_STATE
D_FF = 5632
RMS_EPS = 1e-6

V7X_SUBLANES = 8
V7X_LANES = 128
STATE_ROWS = N_STATES // V7X_LANES
CH_BLOCKS = SSM_WIDTH // V7X_LANES
GROUPS_PER_BLOCK = V7X_LANES // SSM_GROUP
STATES_PER_BLOCK = GROUPS_PER_BLOCK * SSM_STATE
TILES_PER_BLOCK = STATES_PER_BLOCK // V7X_LANES


def _cparams(sem, vmem_mb):
    return pltpu.CompilerParams(dimension_semantics=sem, vmem_limit_bytes=vmem_mb * 1024 * 1024)


def _sigmoid(x):
    return 1.0 / (1.0 + jnp.exp(-x))


def _silu(x):
    return x * _sigmoid(x)


def _gelu_tanh(x):
    c = np.float32(np.sqrt(2.0 / np.pi))
    return x * (0.5 * (1.0 + jnp.tanh(c * (x + np.float32(0.044715) * (x * x * x)))))


def _rms(x):
    return x * lax.rsqrt(jnp.mean(x * x, axis=-1, keepdims=True) + RMS_EPS)


def _modulated_norm(x, g, scale, shift):
    return _rms(x) * (g * (1.0 + scale)) + shift


N_MOD = 6
MOD_ROWS = V7X_SUBLANES


def _ada_kernel(c_ref, w_ref, b_ref, o_ref):
    c = c_ref[...]
    cond = jnp.concatenate([_silu(c), jnp.zeros((MOD_ROWS - c.shape[0], c.shape[1]), F32)], axis=0)
    o_ref[...] = jnp.dot(cond.astype(BF16), w_ref[...].astype(BF16), preferred_element_type=F32) + b_ref[...]


def _ada(c, w, b):
    assert c.shape[0] <= MOD_ROWS and w.shape[1] == N_MOD * D_MODEL
    tn = 1024
    per = D_MODEL // tn
    return pl.pallas_call(
        _ada_kernel,
        grid=(N_MOD * per,),
        in_specs=[pl.BlockSpec(c.shape, lambda j: (0, 0)),
                  pl.BlockSpec((D_MODEL, tn), lambda j: (0, j)),
                  pl.BlockSpec((1, tn), lambda j: (0, j))],
        out_specs=pl.BlockSpec((None, MOD_ROWS, tn), lambda j: (j // per, 0, j % per)),
        out_shape=jax.ShapeDtypeStruct((N_MOD, MOD_ROWS, D_MODEL), F32),
        compiler_params=_cparams(("arbitrary",), 40),
        name="ada",
    )(c, w, b)


def _mod_spec():
    return pl.BlockSpec((N_MOD, MOD_ROWS, D_MODEL), lambda *_: (0, 0, 0))


def _mod_row(mod_ref, k, b):
    return mod_ref[k, pl.ds(b, 1), :]


def _inproj_kernel(x_ref, mod_ref, g_ref, w_ref, wgate_ref, h_ref, qkv_ref, u_ref, wgate_bf_ref, wb_scr,
                   *, tiles_per_seq):
    @pl.when(pl.program_id(0) == 0)
    def _():
        wb_scr[...] = w_ref[...].astype(BF16)

    wgate_bf_ref[...] = wgate_ref[...].astype(BF16)

    b = pl.program_id(0) // tiles_per_seq
    h = _modulated_norm(x_ref[...], g_ref[...], _mod_row(mod_ref, 1, b), _mod_row(mod_ref, 0, b)).astype(BF16)
    h_ref[...] = h
    p = jnp.dot(h, wb_scr[...], preferred_element_type=F32)
    qkv_ref[...] = p[:, :QKV_WIDTH].astype(BF16)
    u_ref[...] = p[:, QKV_WIDTH:]


def _inproj(x2, mod, g, w_in, seq):
    t = x2.shape[0]
    tm = 512
    tpb = seq // tm
    n = QKV_WIDTH + SSM_WIDTH
    steps = t // tm
    gate_cols = w_in.shape[1] - n
    gw = gate_cols // steps
    assert gate_cols % steps == 0 and n % gw == 0 and gw % V7X_LANES == 0
    return pl.pallas_call(
        functools.partial(_inproj_kernel, tiles_per_seq=tpb),
        grid=(steps,),
        in_specs=[pl.BlockSpec((tm, D_MODEL), lambda i: (i, 0)),
                  _mod_spec(),
                  pl.BlockSpec((1, D_MODEL), lambda i: (0, 0)),
                  pl.BlockSpec((D_MODEL, n), lambda i: (0, 0), pipeline_mode=pl.Buffered(1)),
                  pl.BlockSpec((D_MODEL, gw), lambda i: (0, n // gw + i))],
        out_specs=[pl.BlockSpec((tm, D_MODEL), lambda i: (i, 0)),
                   pl.BlockSpec((tm, QKV_WIDTH), lambda i: (i, 0)),
                   pl.BlockSpec((tm, SSM_WIDTH), lambda i: (i, 0)),
                   pl.BlockSpec((D_MODEL, gw), lambda i: (0, i))],
        out_shape=[jax.ShapeDtypeStruct((t, D_MODEL), BF16),
                   jax.ShapeDtypeStruct((t, QKV_WIDTH), BF16),
                   jax.ShapeDtypeStruct((t, SSM_WIDTH), F32),
                   jax.ShapeDtypeStruct((D_MODEL, gate_cols), BF16)],
        scratch_shapes=[pltpu.VMEM((D_MODEL, n), BF16)],
        compiler_params=_cparams(("arbitrary",), 60),
        name="inproj",
    )(x2, mod, g, w_in, w_in)


def _attn_kernel(sink_ref, q_ref, kvc_ref, kvp_ref, w0_ref, w1_ref, w2_ref, o_ref, w0b_ref, w1b_ref, w2b_ref,
                 s_scr, p_scr, t_scr, *, tq, tiles_per_seq):
    for src, dst in ((w0_ref, w0b_ref), (w1_ref, w1b_ref), (w2_ref, w2b_ref)):
        dst[...] = src[...].astype(BF16)

    first = (pl.program_id(0) % tiles_per_seq) == 0
    qi = lax.broadcasted_iota(jnp.int32, (BLOCK, BLOCK), 0)
    kj = lax.broadcasted_iota(jnp.int32, (BLOCK, BLOCK), 1)
    upper = kj > qi
    pairs = Q_PER_KV // 2
    pair_w = 2 * HEAD_DIM
    zeros = jnp.zeros((2 * BLOCK, HEAD_DIM), BF16)
    ones = jnp.ones((2 * BLOCK, pair_w), BF16)
    scale = jnp.asarray(HEAD_DIM ** -0.5, BF16)
    contract_lanes = (((1,), (1,)), ((), ()))
    n_qb = tq // BLOCK

    def window(qb):
        r0 = qb * BLOCK
        prev = kvp_ref[...] if qb == 0 else kvc_ref[r0 - BLOCK:r0, :]
        return jnp.concatenate([prev, kvc_ref[r0:r0 + BLOCK, :]], axis=0)

    def slot(qb, g, par):
        return (qb * N_KV_HEADS + g) * 2 + par

    for qb in range(n_qb):
        r0 = qb * BLOCK
        win = window(qb)
        for g in range(N_KV_HEADS):
            kg = win[:, g * HEAD_DIM:(g + 1) * HEAD_DIM]
            kz = (jnp.concatenate([kg, zeros], axis=1), jnp.concatenate([zeros, kg], axis=1))
            c0 = g * Q_PER_KV * HEAD_DIM
            qg = q_ref[r0:r0 + BLOCK, c0:c0 + Q_PER_KV * HEAD_DIM] * scale
            qs = jnp.concatenate([qg[:, p * pair_w:(p + 1) * pair_w] for p in range(pairs)], axis=0)
            for par in range(2):
                s_scr[slot(qb, g, par)] = lax.dot_general(qs, kz[par], contract_lanes, preferred_element_type=F32)

    for qb in range(n_qb):
        pad_bias = jnp.where(first, NEG_INF, 0.0).astype(F32) if qb == 0 else None
        for g in range(N_KV_HEADS):
            for par in range(2):
                sl = slot(qb, g, par)
                for p in range(pairs):
                    rows = slice(p * BLOCK, (p + 1) * BLOCK)
                    s_prev = s_scr[sl, rows, :BLOCK]
                    if pad_bias is not None:
                        s_prev = s_prev + pad_bias
                    sm = jnp.where(upper, s_prev, s_scr[sl, rows, BLOCK:])
                    sink = sink_ref[g * Q_PER_KV + 2 * p + par]
                    m = jnp.maximum(jnp.max(sm, axis=-1, keepdims=True), sink)
                    m = jnp.broadcast_to(m, (BLOCK, BLOCK))
                    pe = jnp.exp(sm - m)
                    p_scr[sl, rows, :BLOCK] = jnp.where(upper, pe, 0.0).astype(BF16)
                    p_scr[sl, rows, BLOCK:] = jnp.where(upper, 0.0, pe).astype(BF16)
                    t_scr[sl, rows, :] = jnp.exp(sink - m)

    for qb in range(n_qb):
        r0 = qb * BLOCK
        win = window(qb)
        for g in range(N_KV_HEADS):
            vg = win[:, KV_WIDTH + g * HEAD_DIM:KV_WIDTH + (g + 1) * HEAD_DIM]
            vz = (jnp.concatenate([vg, zeros, ones], axis=1), jnp.concatenate([zeros, vg, ones], axis=1))
            c0 = g * Q_PER_KV * HEAD_DIM
            acc = None
            for par in range(2):
                sl = slot(qb, g, par)
                res = jnp.dot(p_scr[sl], vz[par], preferred_element_type=F32)
                on = res[:, :pair_w] / (res[:, pair_w:] + t_scr[sl])
                acc = on if acc is None else acc + on
            for p in range(pairs):
                o_ref[r0:r0 + BLOCK, c0 + p * pair_w:c0 + (p + 1) * pair_w] = (
                    acc[p * BLOCK:(p + 1) * BLOCK].astype(BF16))


def _attention(sinks, qkv, seq, weights):
    t = qkv.shape[0]
    tq = 1024
    tiles_per_seq = seq // tq
    kv_col = ATTN_WIDTH // (2 * KV_WIDTH)
    rb = tq // BLOCK
    steps = t // tq
    slots = rb * N_KV_HEADS * 2
    rows = (Q_PER_KV // 2) * BLOCK
    for w in weights:
        assert w.shape[1] % (steps * V7X_LANES) == 0
    w_specs = [pl.BlockSpec((w.shape[0], w.shape[1] // steps), lambda i: (0, i)) for w in weights]
    return pl.pallas_call(
        functools.partial(_attn_kernel, tq=tq, tiles_per_seq=tiles_per_seq),
        grid=(steps,),
        in_specs=[pl.BlockSpec(memory_space=pltpu.SMEM),
                  pl.BlockSpec((tq, ATTN_WIDTH), lambda i: (i, 0)),
                  pl.BlockSpec((tq, 2 * KV_WIDTH), lambda i: (i, kv_col)),
                  pl.BlockSpec((BLOCK, 2 * KV_WIDTH), lambda i: (jnp.maximum(i * rb - 1, 0), kv_col))] + w_specs,
        out_specs=[pl.BlockSpec((tq, ATTN_WIDTH), lambda i: (i, 0))] + w_specs,
        out_shape=[jax.ShapeDtypeStruct((t, ATTN_WIDTH), BF16)]
                  + [jax.ShapeDtypeStruct(w.shape, BF16) for w in weights],
        scratch_shapes=[pltpu.VMEM((slots, rows, 2 * BLOCK), F32),
                        pltpu.VMEM((slots, rows, 2 * BLOCK), BF16),
                        pltpu.VMEM((slots, rows, BLOCK), F32)],
        compiler_params=_cparams(("arbitrary",), 60),
        name="attn",
    )(sinks, qkv, qkv, qkv, *weights)


def _ssmprep_kernel(ar_ref, ai_ref, ldt_ref, br_ref, bi_ref, cr_ref, ci_ref, lr_ref, li_ref, bw_ref, cw_ref):
    ar = ar_ref[...]
    ai = ai_ref[...]
    dt = jnp.exp(ldt_ref[...])
    mag = jnp.exp(ar * dt)
    lr = mag * jnp.cos(ai * dt)
    li = mag * jnp.sin(ai * dt)
    den = ar * ar + ai * ai
    zr = ((lr - 1.0) * ar + li * ai) / den
    zi = (li * ar - (lr - 1.0) * ai) / den
    lr_ref[...] = lr
    li_ref[...] = li

    def same_group(shape, row_w, col_w):
        rows = lax.broadcasted_iota(jnp.int32, shape, 0) // row_w
        cols = lax.broadcasted_iota(jnp.int32, shape, 1) // col_w
        return rows == cols

    b_mask = same_group((V7X_LANES, STATES_PER_BLOCK), SSM_GROUP, SSM_STATE)
    c_mask = same_group((STATES_PER_BLOCK, V7X_LANES), SSM_STATE, SSM_GROUP)
    for cb in range(CH_BLOCKS):
        gs = slice(cb * GROUPS_PER_BLOCK, (cb + 1) * GROUPS_PER_BLOCK)
        zr_b = zr[gs][:, None, :]
        zi_b = zi[gs][:, None, :]
        br = br_ref[gs]
        bi = bi_ref[gs]
        for part, bbar in enumerate((zr_b * br - zi_b * bi, zr_b * bi + zi_b * br)):
            rows = bbar.reshape(V7X_LANES, SSM_STATE)
            tiled = jnp.concatenate([rows] * GROUPS_PER_BLOCK, axis=1)
            bw_ref[cb, :, part * STATES_PER_BLOCK:(part + 1) * STATES_PER_BLOCK] = (
                jnp.where(b_mask, tiled, 0.0).astype(BF16))
        for part, c_ref in enumerate((cr_ref, ci_ref)):
            rows = c_ref[gs].reshape(STATES_PER_BLOCK, SSM_GROUP)
            tiled = jnp.concatenate([rows] * GROUPS_PER_BLOCK, axis=1)
            signed = tiled if part == 0 else -tiled
            cw_ref[cb, part * STATES_PER_BLOCK:(part + 1) * STATES_PER_BLOCK, :] = (
                jnp.where(c_mask, signed, 0.0).astype(BF16))


def _ssmprep(a_re, a_im, log_dt, b_re_t, b_im_t, c_re_t, c_im_t):
    g, n = a_re.shape
    return pl.pallas_call(
        _ssmprep_kernel,
        out_shape=[jax.ShapeDtypeStruct((g, n), F32), jax.ShapeDtypeStruct((g, n), F32),
                   jax.ShapeDtypeStruct((CH_BLOCKS, V7X_LANES, 2 * STATES_PER_BLOCK), BF16),
                   jax.ShapeDtypeStruct((CH_BLOCKS, 2 * STATES_PER_BLOCK, V7X_LANES), BF16)],
        name="ssmprep",
    )(a_re, a_im, log_dt, b_re_t, b_im_t, c_re_t, c_im_t)


SLAB_PITCH = 2 * STATE_ROWS + 4


def _ssm_kernel(u_ref, bw_ref, cw_ref, lr_ref, li_ref, d_ref, wsrc_ref, gy_ref, wdst_ref, hcr_ref, hci_ref, *bufs,
                tc, nb):
    tiles = tc // V7X_SUBLANES
    wdst_ref[...] = wsrc_ref[...].astype(BF16)

    @pl.when(pl.program_id(0) == 0)
    def _():
        hcr_ref[...] = jnp.zeros_like(hcr_ref)
        hci_ref[...] = jnp.zeros_like(hci_ref)

    def tile_rows(ti, state_tile, imag):
        return pl.ds(ti * V7X_SUBLANES * SLAB_PITCH + 2 * state_tile + imag, V7X_SUBLANES, stride=SLAB_PITCH)

    def slab_rows(t, imag):
        return pl.ds(t * SLAB_PITCH + imag, STATE_ROWS, stride=2)

    def project_in(b):
        x_ref = bufs[2 * b]
        ub = u_ref[b].astype(BF16)
        for cb in range(CH_BLOCKS):
            res = jnp.dot(ub[:, cb * V7X_LANES:(cb + 1) * V7X_LANES], bw_ref[cb], preferred_element_type=F32)
            for k in range(TILES_PER_BLOCK):
                for ti in range(tiles):
                    rs = slice(ti * V7X_SUBLANES, (ti + 1) * V7X_SUBLANES)
                    st = cb * TILES_PER_BLOCK + k
                    x_ref[tile_rows(ti, st, 0), :] = res[rs, k * V7X_LANES:(k + 1) * V7X_LANES]
                    x_ref[tile_rows(ti, st, 1), :] = res[rs, STATES_PER_BLOCK + k * V7X_LANES:
                                                         STATES_PER_BLOCK + (k + 1) * V7X_LANES]

    def recur(b):
        x_ref, h_ref = bufs[2 * b], bufs[2 * b + 1]
        lr = lr_ref[...]
        li = li_ref[...]
        hr = hcr_ref[b]
        hi = hci_ref[b]
        for t in range(tc):
            nr = lr * hr - li * hi + x_ref[slab_rows(t, 0), :]
            ni = lr * hi + li * hr + x_ref[slab_rows(t, 1), :]
            h_ref[slab_rows(t, 0), :] = nr
            h_ref[slab_rows(t, 1), :] = ni
            hr, hi = nr, ni
        hcr_ref[b] = hr
        hci_ref[b] = hi

    def project_out(b):
        h_ref = bufs[2 * b + 1]
        u = u_ref[b]
        for cb in range(CH_BLOCKS):
            row_tiles = []
            for ti in range(tiles):
                parts = [h_ref[tile_rows(ti, cb * TILES_PER_BLOCK + k, imag), :]
                         for imag in (0, 1) for k in range(TILES_PER_BLOCK)]
                row_tiles.append(jnp.concatenate(parts, axis=-1))
            hcat = jnp.concatenate(row_tiles, axis=0).astype(BF16)
            y = jnp.dot(hcat, cw_ref[cb], preferred_element_type=F32)
            cols = slice(cb * V7X_LANES, (cb + 1) * V7X_LANES)
            y = y + d_ref[:, cols] * u[:, cols]
            gy_ref[b, :, cols] = _gelu_tanh(y).astype(BF16)

    for b in range(nb):
        project_in(b)
    for b in range(nb):
        recur(b)
        project_out(b)


def _ssm(u3, bw, cw, lam_r, lam_i, dskip, weight):
    nb, seq, _ = u3.shape
    tc = 256
    steps = seq // tc
    wrows = weight.shape[0] // steps
    assert weight.shape[0] % steps == 0 and wrows % (2 * V7X_SUBLANES) == 0
    w_spec = pl.BlockSpec((wrows, weight.shape[1]), lambda j: (j, 0))
    slab_buf = pltpu.VMEM((tc * SLAB_PITCH, V7X_LANES), F32)
    return pl.pallas_call(
        functools.partial(_ssm_kernel, tc=tc, nb=nb),
        grid=(steps,),
        in_specs=[pl.BlockSpec((nb, tc, SSM_WIDTH), lambda j: (0, j, 0)),
                  pl.BlockSpec(bw.shape, lambda j: (0, 0, 0)),
                  pl.BlockSpec(cw.shape, lambda j: (0, 0, 0)),
                  pl.BlockSpec((STATE_ROWS, V7X_LANES), lambda j: (0, 0)),
                  pl.BlockSpec((STATE_ROWS, V7X_LANES), lambda j: (0, 0)),
                  pl.BlockSpec((1, SSM_WIDTH), lambda j: (0, 0)),
                  w_spec],
        out_specs=[pl.BlockSpec((nb, tc, SSM_WIDTH), lambda j: (0, j, 0)), w_spec],
        out_shape=[jax.ShapeDtypeStruct((nb, seq, SSM_WIDTH), BF16), jax.ShapeDtypeStruct(weight.shape, BF16)],
        scratch_shapes=[pltpu.VMEM((nb, STATE_ROWS, V7X_LANES), F32),
                        pltpu.VMEM((nb, STATE_ROWS, V7X_LANES), F32)] + [slab_buf] * (2 * nb),
        compiler_params=_cparams(("arbitrary",), 60),
        name="ssm",
    )(u3, bw, cw, lam_r, lam_i, dskip, weight)


def _mix_kernel(x_ref, h_ref, mod_ref, g2n_ref, o_ref, gy_ref, wga_ref, wgs_ref, wap_ref, wla_ref, wlb_ref,
                wout_ref, wsrc_ref, x1_ref, h2_ref, wdst_ref, *, tiles_per_seq):
    n = pl.program_id(1)
    b = pl.program_id(0) // tiles_per_seq
    wdst_ref[...] = wsrc_ref[...].astype(BF16)

    def partial_out():
        h = h_ref[...]
        ga = jnp.dot(h, wga_ref[...], preferred_element_type=F32)
        gs = jnp.dot(h, wgs_ref[...], preferred_element_type=F32)
        at = jnp.dot(o_ref[...], wap_ref[...], preferred_element_type=F32)
        gy = gy_ref[...]
        la = jnp.dot(gy, wla_ref[...], preferred_element_type=F32)
        lb = jnp.dot(gy, wlb_ref[...], preferred_element_type=F32)
        ssm = la * _sigmoid(lb)
        mixed = _sigmoid(ga) * at + _sigmoid(gs) * ssm
        return jnp.dot(mixed.astype(BF16), wout_ref[...], preferred_element_type=F32)

    @pl.when(n == 0)
    def _():
        x1_ref[...] = partial_out()

    @pl.when(n > 0)
    def _():
        x1_ref[...] += partial_out()

    @pl.when(n == pl.num_programs(1) - 1)
    def _():
        g1 = _mod_row(mod_ref, 2, b)
        geff = g2n_ref[...] * (1.0 + _mod_row(mod_ref, 4, b))
        shift = _mod_row(mod_ref, 3, b)
        chunk = 2 * V7X_SUBLANES
        for r in range(x1_ref.shape[0] // chunk):
            rows = slice(r * chunk, (r + 1) * chunk)
            x1 = x_ref[rows, :] + g1 * x1_ref[rows, :]
            x1_ref[rows, :] = x1
            h2_ref[rows, :] = (_rms(x1) * geff + shift).astype(BF16)


def _mix(x2, h, mod, g2n, o, gy3, wgates, wap, wglu, wout, seq, weight):
    t = x2.shape[0]
    tm, tn = 512, 512
    tpb = seq // tm
    nn = D_MODEL // tn
    steps = (t // tm) * nn
    wrows = weight.shape[0] // steps
    assert weight.shape[0] % steps == 0 and wrows % (2 * V7X_SUBLANES) == 0
    w_spec = pl.BlockSpec((wrows, weight.shape[1]), lambda i, n: (i * nn + n, 0))
    return pl.pallas_call(
        functools.partial(_mix_kernel, tiles_per_seq=tpb),
        grid=(t // tm, nn),
        in_specs=[pl.BlockSpec((tm, D_MODEL), lambda i, n: (jnp.where(n >= 1, i, jnp.maximum(i - 1, 0)), 0)),
                  pl.BlockSpec((tm, D_MODEL), lambda i, n: (i, 0)),
                  _mod_spec(),
                  pl.BlockSpec((1, D_MODEL), lambda i, n: (0, 0)),
                  pl.BlockSpec((tm, ATTN_WIDTH), lambda i, n: (i, 0)),
                  pl.BlockSpec((None, tm, SSM_WIDTH), lambda i, n: (i // tpb, i % tpb, 0)),
                  pl.BlockSpec((D_MODEL, tn), lambda i, n: (0, n)),
                  pl.BlockSpec((D_MODEL, tn), lambda i, n: (0, nn + n)),
                  pl.BlockSpec((ATTN_WIDTH, tn), lambda i, n: (0, n)),
                  pl.BlockSpec((SSM_WIDTH, tn), lambda i, n: (0, n)),
                  pl.BlockSpec((SSM_WIDTH, tn), lambda i, n: (0, nn + n)),
                  pl.BlockSpec((tn, D_MODEL), lambda i, n: (n, 0)),
                  w_spec],
        out_specs=[pl.BlockSpec((tm, D_MODEL), lambda i, n: (i, 0)),
                   pl.BlockSpec((tm, D_MODEL), lambda i, n: (i, 0)),
                   w_spec],
        out_shape=[jax.ShapeDtypeStruct((t, D_MODEL), F32),
                   jax.ShapeDtypeStruct((t, D_MODEL), BF16),
                   jax.ShapeDtypeStruct(weight.shape, BF16)],
        compiler_params=_cparams(("arbitrary", "arbitrary"), 60),
        name="mix",
    )(x2, h, mod, g2n, o, gy3, wgates, wgates, wap, wglu, wglu, wout, weight)


def _ffn_kernel(h_ref, halo_ref, x1_ref, mod_ref, wg_ref, wv_ref, cw_ref, cb_ref, wd_ref, gf_ref, out_ref,
                act_even, act_odd, *, tm, n_tiles, tiles_per_seq, final_norm):
    f = pl.program_id(1)
    first = (pl.program_id(0) % tiles_per_seq) == 0
    act_bufs = (act_even, act_odd)

    def down_stage(act_ref):
        out_ref[...] += jnp.dot(act_ref[...], wd_ref[...], preferred_element_type=F32)

    def step(act_out, act_in):
        h = h_ref[...]
        wg = wg_ref[...]
        gate = jnp.dot(h, wg, preferred_element_type=F32)
        val = jnp.dot(h, wv_ref[...], preferred_element_type=F32)
        halo = jnp.dot(halo_ref[...], wg, preferred_element_type=F32)
        if act_in is not None:
            down_stage(act_in)
        ext = jnp.concatenate([jnp.where(first, 0.0, halo), gate], axis=0)
        g1 = pltpu.roll(ext, 1, 0)[V7X_SUBLANES:]
        g2 = pltpu.roll(ext, 2, 0)[V7X_SUBLANES:]
        conv = cw_ref[0:1, :] * g2 + cw_ref[1:2, :] * g1 + cw_ref[2:3, :] * gate + cb_ref[...]
        act_out[...] = (_silu(conv) * val).astype(BF16)

    @pl.when(f == 0)
    def _():
        out_ref[...] = jnp.zeros_like(out_ref)
        step(act_bufs[0], None)

    for parity in range(2):
        @pl.when((f > 0) & (f < n_tiles) & (f % 2 == parity))
        def _():
            step(act_bufs[parity], act_bufs[1 - parity])

    @pl.when(f == n_tiles)
    def _():
        down_stage(act_bufs[(n_tiles - 1) % 2])
        x2 = x1_ref[...] + _mod_row(mod_ref, 5, pl.program_id(0) // tiles_per_seq) * out_ref[...]
        out_ref[...] = _rms(x2) * gf_ref[...] if final_norm else x2


def _ffn(h2, x1, mod, wup, conv_w, conv_b, wdown, gf, seq, final_norm):
    t = h2.shape[0]
    tm, tf = 1024, 512
    tpb = seq // tm
    nf = D_FF // tf
    hb = tm // V7X_SUBLANES
    return pl.pallas_call(
        functools.partial(_ffn_kernel, tm=tm, n_tiles=nf, tiles_per_seq=tpb, final_norm=final_norm),
        grid=(t // tm, nf + 1),
        in_specs=[pl.BlockSpec((tm, D_MODEL), lambda i, f: (i, 0)),
                  pl.BlockSpec((V7X_SUBLANES, D_MODEL), lambda i, f: (jnp.maximum(i * hb - 1, 0), 0)),
                  pl.BlockSpec((tm, D_MODEL), lambda i, f: (jnp.where(f >= 4, i, jnp.maximum(i - 1, 0)), 0)),
                  _mod_spec(),
                  pl.BlockSpec((D_MODEL, tf), lambda i, f: (0, jnp.minimum(f, nf - 1))),
                  pl.BlockSpec((D_MODEL, tf), lambda i, f: (0, nf + jnp.minimum(f, nf - 1))),
                  pl.BlockSpec((3, tf), lambda i, f: (0, jnp.minimum(f, nf - 1))),
                  pl.BlockSpec((1, tf), lambda i, f: (0, jnp.minimum(f, nf - 1))),
                  pl.BlockSpec((tf, D_MODEL), lambda i, f: (jnp.maximum(f - 1, 0), 0)),
                  pl.BlockSpec((1, D_MODEL), lambda i, f: (0, 0))],
        out_specs=pl.BlockSpec((tm, D_MODEL), lambda i, f: (i, 0)),
        out_shape=jax.ShapeDtypeStruct((t, D_MODEL), F32),
        scratch_shapes=[pltpu.VMEM((tm, tf), BF16), pltpu.VMEM((tm, tf), BF16)],
        compiler_params=_cparams(("arbitrary", "arbitrary"), 63),
        name="ffn",
    )(h2, h2, x1, mod, wup, wup, conv_w, conv_b, wdown, gf)


def kernel(x, c, ada_w, ada_b, norm_mix_g, w_in, attn_sinks, w_attn_proj, ssm_a_re, ssm_a_im, ssm_log_dt,
           ssm_b_re, ssm_b_im, ssm_c_re, ssm_c_im, ssm_d, w_ssm_glu, w_out, norm_ffn_g, w_ffn_up, ffn_conv_w,
           ffn_conv_b, w_ffn_down, final_g):
    bsz, seq, _ = x.shape
    x2 = x.reshape(bsz * seq, D_MODEL)
    for i in range(ada_w.shape[0]):
        mod = _ada(c, ada_w[i], ada_b[i].reshape(1, -1))

        win = w_in[i]
        g1n = norm_mix_g[i].reshape(1, D_MODEL)
        g2n = norm_ffn_g[i].reshape(1, D_MODEL)

        h, qkv, u, w_gates = _inproj(x2, mod, g1n, win, seq)
        o, w_ap, w_glu, w_o = _attention(attn_sinks[i], qkv, seq, (w_attn_proj[i], w_ssm_glu[i], w_out[i]))

        lam_r, lam_i, bw, cw = _ssmprep(
            ssm_a_re[i], ssm_a_im[i], ssm_log_dt[i].reshape(-1, 1),
            ssm_b_re[i].transpose(0, 2, 1), ssm_b_im[i].transpose(0, 2, 1),
            ssm_c_re[i].transpose(0, 2, 1), ssm_c_im[i].transpose(0, 2, 1))
        gy, w_down = _ssm(u.reshape(bsz, seq, SSM_WIDTH), bw, cw,
                          lam_r.reshape(STATE_ROWS, V7X_LANES), lam_i.reshape(STATE_ROWS, V7X_LANES),
                          ssm_d[i].reshape(1, SSM_WIDTH), w_ffn_down[i])

        x2, h2, w_up = _mix(x2, h, mod, g2n, o, gy, w_gates, w_ap, w_glu, w_o, seq, w_ffn_up[i])
        x2 = _ffn(h2, x2, mod, w_up, ffn_conv_w[i], ffn_conv_b[i].reshape(1, D_FF),
                  w_down, final_g.reshape(1, D_MODEL), seq, i == ada_w.shape[0] - 1)
    return x2.reshape(bsz, seq, D_MODEL)
```
